```python
import math
import jax, jax.numpy as jnp
from jax import lax
import numpy as np

D_MODEL = 1024
BATCH = 4
SEQ = 8192
DEPTH = 1

GMLP_WIDTH = D_MODEL
CHUNK = 128
GMLP_GROUPS = 8
GMLP_GROUP_CH = GMLP_WIDTH // GMLP_GROUPS
DIFF_HEADS = 8
DIFF_VDIM = D_MODEL // DIFF_HEADS
DIFF_QK_DIM = DIFF_VDIM // 2
ATTN_WIDTH = DIFF_HEADS * DIFF_VDIM
Q_BLOCK = 128
ROPE_THETA = 10000.0
D_FF = 4 * D_MODEL
IN_COLS = 2 * GMLP_WIDTH + 3 * ATTN_WIDTH + 2 * D_MODEL
N_MOD = 6
EPS = 1e-6

kernel_name = "hybrid_gmlp_diffattn_block"


def rmsnorm(x, g):
    xf = x.astype(jnp.float32)
    y = xf * lax.rsqrt(jnp.mean(xf * xf, axis=-1, keepdims=True) + EPS)
    return (y * g.astype(jnp.float32)).astype(x.dtype)


def layernorm(x, g, b):
    xf = x.astype(jnp.float32)
    mu = jnp.mean(xf, axis=-1, keepdims=True)
    xc = xf - mu
    y = xc * lax.rsqrt(jnp.mean(xc * xc, axis=-1, keepdims=True) + EPS)
    return (y * g.astype(jnp.float32) + b.astype(jnp.float32)).astype(x.dtype)


def rope_tables(positions):
    inv_freq = ROPE_THETA ** (-jnp.arange(0, DIFF_QK_DIM, 2, dtype=jnp.float32) / DIFF_QK_DIM)
    ang = positions.astype(jnp.float32)[..., None] * inv_freq
    return jnp.cos(ang)[:, :, None, None, :], jnp.sin(ang)[:, :, None, None, :]


def apply_rope(t, cos, sin):
    half = DIFF_QK_DIM // 2
    tf = t.astype(jnp.float32)
    t1, t2 = tf[..., :half], tf[..., half:]
    out = jnp.concatenate([t1 * cos - t2 * sin, t2 * cos + t1 * sin], axis=-1)
    return out.astype(t.dtype)


def gmlp_mixer(u, v, ln_g, ln_b, w_s, b_s):
    B, S, _ = u.shape
    u = jax.nn.gelu(u)
    v = layernorm(jax.nn.gelu(v), ln_g, ln_b)
    vc = v.reshape(B, S // CHUNK, CHUNK, GMLP_GROUPS, GMLP_GROUP_CH)
    sv = jnp.einsum('gpq,bnqgc->bnpgc', w_s, vc) + b_s.T[None, None, :, :, None]
    return u * sv.reshape(B, S, GMLP_WIDTH)


def diff_attention(q, k, v, cos, sin, lam, subln_g, lambda_init):
    B, S = q.shape[0], q.shape[1]
    scale = DIFF_QK_DIM ** -0.5
    q = apply_rope(q, cos, sin) * scale
    k = apply_rope(k, cos, sin)
    nb = S // Q_BLOCK
    qb = q.reshape(B, nb, Q_BLOCK, DIFF_HEADS, 2, DIFF_QK_DIM).transpose(1, 0, 2, 3, 4, 5)

    def attend(q_blk):
        s = jnp.einsum('bqhmd,bkhmd->bhmqk', q_blk, k).astype(jnp.float32)
        p = jax.nn.softmax(s, axis=-1)
        a = p[:, :, 0] - lam * p[:, :, 1]
        return jnp.einsum('bhqk,bkhe->bqhe', a.astype(v.dtype), v)

    o = lax.map(attend, qb)
    o = o.transpose(1, 0, 2, 3, 4).reshape(B, S, DIFF_HEADS, DIFF_VDIM)
    o = rmsnorm(o, subln_g) * (1.0 - lambda_init)
    return o.reshape(B, S, ATTN_WIDTH)


def setup_inputs(seed: int = 0) -> dict:
    key = jax.random.key(seed)
    ks = jax.random.split(key, 24)
    f32 = jnp.float32
    L, D = DEPTH, D_MODEL
    nrm = lambda k, shape, s: jax.random.normal(k, shape, f32) * s
    x = nrm(ks[0], (BATCH, SEQ, D), 1.0)
    c = nrm(ks[1], (BATCH, D), 1.0)
    offs = jax.random.randint(ks[2], (BATCH, 1), 0, SEQ, dtype=jnp.int32)
    positions = (jnp.arange(SEQ, dtype=jnp.int32)[None, :] + offs).astype(jnp.int32)
    return {
        "x": x,
        "c": c,
        "positions": positions,
        "w_ada": nrm(ks[3], (L, D, N_MOD * D), D ** -0.5),
        "b_ada": nrm(ks[4], (L, N_MOD * D), 0.01),
        "g_norm1": 1.0 + nrm(ks[5], (L, D), 0.02),
        "w_in": nrm(ks[6], (L, D, IN_COLS), D ** -0.5),
        "gmlp_ln_g": 1.0 + nrm(ks[7], (L, GMLP_WIDTH), 0.02),
        "gmlp_ln_b": nrm(ks[8], (L, GMLP_WIDTH), 0.01),
        "w_spatial": nrm(ks[9], (L, GMLP_GROUPS, CHUNK, CHUNK), CHUNK ** -0.5),
        "b_spatial": 1.0 + nrm(ks[10], (L, GMLP_GROUPS, CHUNK), 0.02),
        "lambda_q1": nrm(ks[11], (L, DIFF_QK_DIM), 0.1),
        "lambda_k1": nrm(ks[12], (L, DIFF_QK_DIM), 0.1),
        "lambda_q2": nrm(ks[13], (L, DIFF_QK_DIM), 0.1),
        "lambda_k2": nrm(ks[14], (L, DIFF_QK_DIM), 0.1),
        "subln_g": 1.0 + nrm(ks[15], (L, DIFF_VDIM), 0.02),
        "w_out": nrm(ks[16], (L, D, D), D ** -0.5),
        "g_norm2": 1.0 + nrm(ks[17], (L, D), 0.02),
        "w_ff1": nrm(ks[18], (L, D, D_FF), D ** -0.5),
        "w_ff2": nrm(ks[19], (L, D_FF, D), D_FF ** -0.5),
        "g_final": 1.0 + nrm(ks[20], (D,), 0.02),
    }


def reference(x, c, positions, w_ada, b_ada, g_norm1, w_in, gmlp_ln_g, gmlp_ln_b,
              w_spatial, b_spatial, lambda_q1, lambda_k1, lambda_q2, lambda_k2,
              subln_g, w_out, g_norm2, w_ff1, w_ff2, g_final):
    B, S, D = x.shape
    cos, sin = rope_tables(positions)
    c_act = jax.nn.silu(c)
    col = np.cumsum([0, GMLP_WIDTH, GMLP_WIDTH, ATTN_WIDTH, ATTN_WIDTH, ATTN_WIDTH, D_MODEL, D_MODEL])
    for l in range(DEPTH):
        lambda_init = 0.8 - 0.6 * math.exp(-0.3 * l)
        mod = c_act @ w_ada[l] + b_ada[l]
        sh1, sc1, gt1, sh2, sc2, gt2 = [m[:, None, :] for m in jnp.split(mod, N_MOD, axis=-1)]

        h = rmsnorm(x, g_norm1[l]) * (1.0 + sc1) + sh1
        z = h @ w_in[l]
        u_a = z[..., col[0]:col[1]]
        v_a = z[..., col[1]:col[2]]
        q = z[..., col[2]:col[3]].reshape(B, S, DIFF_HEADS, 2, DIFF_QK_DIM)
        k = z[..., col[3]:col[4]].reshape(B, S, DIFF_HEADS, 2, DIFF_QK_DIM)
        v = z[..., col[4]:col[5]].reshape(B, S, DIFF_HEADS, DIFF_VDIM)
        gate_a = jax.nn.sigmoid(z[..., col[5]:col[6]])
        gate_b = jax.nn.sigmoid(z[..., col[6]:col[7]])

        branch_a = gmlp_mixer(u_a, v_a, gmlp_ln_g[l], gmlp_ln_b[l], w_spatial[l], b_spatial[l])
        lam = (jnp.exp(jnp.sum(lambda_q1[l].astype(jnp.float32) * lambda_k1[l].astype(jnp.float32)))
               - jnp.exp(jnp.sum(lambda_q2[l].astype(jnp.float32) * lambda_k2[l].astype(jnp.float32)))
               + lambda_init)
        branch_b = diff_attention(q, k, v, cos, sin, lam, subln_g[l], lambda_init)

        merged = gate_a * branch_a + gate_b * branch_b
        x = x + gt1 * (merged @ w_out[l])

        h2 = rmsnorm(x, g_norm2[l]) * (1.0 + sc2) + sh2
        ff = jnp.square(jax.nn.relu(h2 @ w_ff1[l])) @ w_ff2[l]
        x = x + gt2 * ff
    return rmsnorm(x, g_final)
```

```python
import functools
import math

import jax
import jax.numpy as jnp
from jax import lax
from jax.experimental import pallas as pl
from jax.experimental.pallas import tpu as pltpu

D_MODEL = 1024
GMLP_GROUPS = 8
CHUNK = 128
DIFF_HEADS = 8
DIFF_VDIM = 128
DIFF_QK_DIM = 64
D_FF = 4 * D_MODEL
N_MOD = 6
N_IN_GROUPS = 7
ROPE_THETA = 10000.0
EPS = 1e-6
LAMBDA_INIT = 0.8 - 0.6 * math.exp(-0.3 * 0)

LANES = 128
TOKEN_TILE = 512
Q_TILE = 512
K_TILE = 512
VMEM_LIMIT = 48 * 1024 * 1024

F32 = jnp.float32
BF16 = jnp.bfloat16


def _rms(x, g):
    return x * lax.rsqrt(jnp.mean(x * x, axis=-1, keepdims=True) + EPS) * g


def _mod_kernel(ct_ref, w_ref, b_ref, lq1_ref, lk1_ref, lq2_ref, lk2_ref, mod_ref, lam_ref):
    ct = ct_ref[...]
    act = ct * jax.nn.sigmoid(ct)
    w = w_ref[...]
    for b in range(ct.shape[1]):
        row = jnp.sum(act[:, b:b + 1] * w, axis=0, keepdims=True)
        mod_ref[b:b + 1, :] = row + b_ref[...]

    @pl.when(pl.program_id(0) == 0)
    def _():
        s1 = jnp.sum(lq1_ref[...] * lk1_ref[...], axis=-1, keepdims=True)
        s2 = jnp.sum(lq2_ref[...] * lk2_ref[...], axis=-1, keepdims=True)
        lam = jnp.exp(s1) - jnp.exp(s2) + LAMBDA_INIT
        lam_ref[...] = jnp.broadcast_to(lam, lam_ref.shape)


def _modulation(c, w_ada, b_ada, lq1, lk1, lq2, lk2):
    B, D = c.shape
    N = w_ada.shape[1]
    tn = 512
    row = lambda a: a.reshape(1, -1)
    small = pl.BlockSpec((1, DIFF_QK_DIM), lambda n: (0, 0))
    return pl.pallas_call(
        _mod_kernel,
        grid=(N // tn,),
        in_specs=[
            pl.BlockSpec((D, B), lambda n: (0, 0)),
            pl.BlockSpec((D, tn), lambda n: (0, n)),
            pl.BlockSpec((1, tn), lambda n: (0, n)),
            small, small, small, small,
        ],
        out_specs=[
            pl.BlockSpec((B, tn), lambda n: (0, n)),
            pl.BlockSpec((1, LANES), lambda n: (0, 0)),
        ],
        out_shape=[
            jax.ShapeDtypeStruct((B, N), F32),
            jax.ShapeDtypeStruct((1, LANES), F32),
        ],
        compiler_params=pltpu.CompilerParams(
            dimension_semantics=("arbitrary",), vmem_limit_bytes=VMEM_LIMIT),
        name="adaln_mod",
    )(c.T, w_ada, row(b_ada), row(lq1), row(lk1), row(lq2), row(lk2))


def _rope(z, cos, sin_signed, lane):
    first_half = (lane % DIFF_QK_DIM) < (DIFF_QK_DIM // 2)
    partner = jnp.where(first_half,
                        pltpu.roll(z, LANES - DIFF_QK_DIM // 2, axis=1),
                        pltpu.roll(z, DIFF_QK_DIM // 2, axis=1))
    return z * cos + partner * sin_signed


def _in_proj_kernel(pos_ref, x_ref, mod_ref, g1_ref, w_ref, lng_ref, lnb_ref, ws_ref, bs_ref,
                    freq_ref, a_ref, gb_ref, q_ref, k_ref, v_ref,
                    h_sc, u_sc, sv_sc, cos_sc, sin_sc):
    j = pl.program_id(1)
    tm = x_ref.shape[0]

    @pl.when(j == 0)
    def _():
        mod = mod_ref[0]
        sh1 = mod[:, 0:D_MODEL]
        sc1 = mod[:, D_MODEL:2 * D_MODEL]
        h = _rms(x_ref[...], g1_ref[...]) * (1.0 + sc1) + sh1
        h_sc[...] = h.astype(BF16)

    def proj():
        return jnp.dot(h_sc[...], w_ref[...], preferred_element_type=F32)

    @pl.when(j == 0)
    def _():
        u_sc[...] = jax.nn.gelu(proj())

    @pl.when(j == 1)
    def _():
        g = jax.nn.gelu(proj())
        mu = jnp.mean(g, axis=-1, keepdims=True)
        gc = g - mu
        var = jnp.mean(gc * gc, axis=-1, keepdims=True)
        vln = (gc * lax.rsqrt(var + EPS) * lng_ref[...] + lnb_ref[...]).astype(BF16)
        for grp in range(GMLP_GROUPS):
            cols = slice(grp * CHUNK, (grp + 1) * CHUNK)
            w_s = ws_ref[grp]
            b_s = bs_ref[grp]
            for n in range(tm // CHUNK):
                rows = slice(n * CHUNK, (n + 1) * CHUNK)
                sv_sc[rows, cols] = jnp.dot(w_s, vln[rows, cols],
                                            preferred_element_type=F32) + b_s

    @pl.when(j == 2)
    def _():
        ang = pos_ref[...] * freq_ref[...]
        lane = lax.broadcasted_iota(jnp.int32, (tm, LANES), 1)
        first_half = (lane % DIFF_QK_DIM) < (DIFF_QK_DIM // 2)
        sin = jnp.sin(ang)
        cos_sc[...] = jnp.cos(ang)
        sin_sc[...] = jnp.where(first_half, -sin, sin)

    def rope_store(out_ref, scale):
        z = proj()
        lane = lax.broadcasted_iota(jnp.int32, (tm, LANES), 1)
        cos = cos_sc[...]
        sin_signed = sin_sc[...]
        for s in range(D_MODEL // LANES):
            cols = slice(s * LANES, (s + 1) * LANES)
            r = _rope(z[:, cols], cos, sin_signed, lane)
            if scale != 1.0:
                r = r * scale
            out_ref[:, cols] = r.astype(out_ref.dtype)

    @pl.when(j == 2)
    def _():
        rope_store(q_ref, DIFF_QK_DIM ** -0.5)

    @pl.when(j == 3)
    def _():
        rope_store(k_ref, 1.0)

    @pl.when(j == 4)
    def _():
        v_ref[...] = proj().astype(v_ref.dtype)

    @pl.when(j == 5)
    def _():
        a_ref[...] = jax.nn.sigmoid(proj()) * (u_sc[...] * sv_sc[...])

    @pl.when(j == 6)
    def _():
        gb_ref[...] = jax.nn.sigmoid(proj())


def _in_proj(pos, x2, mod3, g1, w_in, ln_g, ln_b, w_s, b_s, tiles_per_batch):
    T, D = x2.shape
    tm = TOKEN_TILE
    freq = ROPE_THETA ** (-jnp.arange(0, DIFF_QK_DIM, 2, dtype=F32) / DIFF_QK_DIM)
    freq = jnp.tile(freq, LANES // freq.shape[0]).reshape(1, LANES)
    const2 = lambda i, j: (0, 0)
    const3 = lambda i, j: (0, 0, 0)
    tok = lambda i, j: (i, 0)
    out_tok = pl.BlockSpec((tm, D), tok)
    return pl.pallas_call(
        _in_proj_kernel,
        grid=(T // tm, N_IN_GROUPS),
        in_specs=[
            pl.BlockSpec((tm, 1), tok),
            pl.BlockSpec((tm, D), tok),
            pl.BlockSpec((1, 1, N_MOD * D), lambda i, j: (i // tiles_per_batch, 0, 0)),
            pl.BlockSpec((1, D), const2),
            pl.BlockSpec((D, D), lambda i, j: (0, j)),
            pl.BlockSpec((1, D), const2),
            pl.BlockSpec((1, D), const2),
            pl.BlockSpec((GMLP_GROUPS, CHUNK, CHUNK), const3),
            pl.BlockSpec((GMLP_GROUPS, CHUNK, CHUNK), const3),
            pl.BlockSpec((1, LANES), const2),
        ],
        out_specs=[out_tok, out_tok, out_tok, out_tok, out_tok],
        out_shape=[
            jax.ShapeDtypeStruct((T, D), F32),
            jax.ShapeDtypeStruct((T, D), F32),
            jax.ShapeDtypeStruct((T, D), BF16),
            jax.ShapeDtypeStruct((T, D), BF16),
            jax.ShapeDtypeStruct((T, D), BF16),
        ],
        scratch_shapes=[
            pltpu.VMEM((tm, D), BF16),
            pltpu.VMEM((tm, D), F32),
            pltpu.VMEM((tm, D), F32),
            pltpu.VMEM((tm, LANES), F32),
            pltpu.VMEM((tm, LANES), F32),
        ],
        compiler_params=pltpu.CompilerParams(
            dimension_semantics=("arbitrary", "arbitrary"), vmem_limit_bytes=VMEM_LIMIT),
        name="in_proj",
    )(pos, x2, mod3, g1, w_in, ln_g, ln_b, w_s, b_s, freq)


def _attn_kernel(lam_ref, q_ref, k_ref, v_ref, g_ref, o_ref, q12_sc, m_sc, l_sc, acc_sc):
    tq = q_ref.shape[0]
    n_kt = k_ref.shape[0] // K_TILE

    q = q_ref[...]
    lane = lax.broadcasted_iota(jnp.int32, q.shape, 1)
    zero = jnp.zeros_like(q)
    q12_sc[0:tq, :] = jnp.where(lane < DIFF_QK_DIM, q, zero)
    q12_sc[tq:2 * tq, :] = jnp.where(lane >= DIFF_QK_DIM, q, zero)
    m_sc[...] = jnp.full(m_sc.shape, -jnp.inf, F32)
    l_sc[...] = jnp.zeros(l_sc.shape, F32)
    acc_sc[...] = jnp.zeros(acc_sc.shape, F32)

    def body(kt, carry):
        start = pl.multiple_of(kt * K_TILE, K_TILE)
        k_blk = k_ref[pl.ds(start, K_TILE), :]
        v_blk = v_ref[pl.ds(start, K_TILE), :]
        s = lax.dot_general(q12_sc[...], k_blk, (((1,), (1,)), ((), ())),
                            preferred_element_type=F32)
        m_old = m_sc[...]
        m_new = jnp.maximum(m_old, jnp.max(s, axis=-1, keepdims=True))
        alpha = jnp.exp(m_old - m_new)
        p = jnp.exp(s - m_new[:, 0:1])
        l_sc[...] = alpha * l_sc[...] + jnp.sum(p, axis=-1, keepdims=True)
        acc_sc[...] = alpha * acc_sc[...] + jnp.dot(p.astype(BF16), v_blk,
                                                    preferred_element_type=F32)
        m_sc[...] = m_new
        return carry

    lax.fori_loop(0, n_kt, body, 0)

    o1 = acc_sc[0:tq, :] / l_sc[0:tq, :]
    o2 = acc_sc[tq:2 * tq, :] / l_sc[tq:2 * tq, :]
    o = o1 - lam_ref[...] * o2
    o_ref[...] = (_rms(o, g_ref[...]) * (1.0 - LAMBDA_INIT)).astype(o_ref.dtype)


def _attention(lam, q, k, v, subln_g, B, S):
    T, D = q.shape
    tq = Q_TILE
    nq = S // tq
    hd = DIFF_VDIM
    kv_spec = pl.BlockSpec((S, hd), lambda b, h, i: (b, h))
    return pl.pallas_call(
        _attn_kernel,
        grid=(B, DIFF_HEADS, nq),
        in_specs=[
            pl.BlockSpec((1, LANES), lambda b, h, i: (0, 0)),
            pl.BlockSpec((tq, hd), lambda b, h, i: (b * nq + i, h)),
            kv_spec,
            kv_spec,
            pl.BlockSpec((1, hd), lambda b, h, i: (0, 0)),
        ],
        out_specs=pl.BlockSpec((tq, hd), lambda b, h, i: (b * nq + i, h)),
        out_shape=jax.ShapeDtypeStruct((T, D), F32),
        scratch_shapes=[
            pltpu.VMEM((2 * tq, hd), BF16),
            pltpu.VMEM((2 * tq, LANES), F32),
            pltpu.VMEM((2 * tq, LANES), F32),
            pltpu.VMEM((2 * tq, hd), F32),
        ],
        compiler_params=pltpu.CompilerParams(
            dimension_semantics=("arbitrary", "arbitrary", "arbitrary"),
            vmem_limit_bytes=VMEM_LIMIT),
        name="diff_attn",
    )(lam, q, k, v, subln_g)


def _out_ffn_kernel(x_ref, a_ref, gb_ref, bb_ref, mod_ref, wout_ref, g2_ref, w1_ref, w2_ref,
                    gf_ref, o_ref):
    mod = mod_ref[0]
    gt1 = mod[:, 2 * D_MODEL:3 * D_MODEL]
    sh2 = mod[:, 3 * D_MODEL:4 * D_MODEL]
    sc2 = mod[:, 4 * D_MODEL:5 * D_MODEL]
    gt2 = mod[:, 5 * D_MODEL:6 * D_MODEL]

    merged = a_ref[...] + gb_ref[...] * bb_ref[...]
    x1 = x_ref[...] + gt1 * jnp.dot(merged.astype(BF16), wout_ref[...],
                                    preferred_element_type=F32)
    h2 = (_rms(x1, g2_ref[...]) * (1.0 + sc2) + sh2).astype(BF16)
    ff = None
    for c in range(D_FF // D_MODEL):
        cols = slice(c * D_MODEL, (c + 1) * D_MODEL)
        hid = jnp.dot(h2, w1_ref[:, cols], preferred_element_type=F32)
        hid = jnp.square(jnp.maximum(hid, 0.0)).astype(BF16)
        part = jnp.dot(hid, w2_ref[cols, :], preferred_element_type=F32)
        ff = part if ff is None else ff + part
    x2 = x1 + gt2 * ff
    o_ref[...] = _rms(x2, gf_ref[...])


def _out_ffn(x2, a, gb, bb, mod3, w_out, g2, w1, w2, gf, tiles_per_batch):
    T, D = x2.shape
    tm = TOKEN_TILE
    tok = pl.BlockSpec((tm, D), lambda i: (i, 0))
    const = lambda i: (0, 0)
    resident = lambda shape: pl.BlockSpec(shape, const, pipeline_mode=pl.Buffered(1))
    return pl.pallas_call(
        _out_ffn_kernel,
        grid=(T // tm,),
        in_specs=[
            tok, tok, tok, tok,
            pl.BlockSpec((1, 1, N_MOD * D), lambda i: (i // tiles_per_batch, 0, 0)),
            resident((D, D)),
            pl.BlockSpec((1, D), const),
            resident((D, D_FF)),
            resident((D_FF, D)),
            pl.BlockSpec((1, D), const),
        ],
        out_specs=tok,
        out_shape=jax.ShapeDtypeStruct((T, D), F32),
        compiler_params=pltpu.CompilerParams(
            dimension_semantics=("arbitrary",), vmem_limit_bytes=VMEM_LIMIT),
        name="out_ffn",
    )(x2, a, gb, bb, mod3, w_out, g2, w1, w2, gf)


def kernel(x, c, positions, w_ada, b_ada, g_norm1, w_in, gmlp_ln_g, gmlp_ln_b, w_spatial,
           b_spatial, lambda_q1, lambda_k1, lambda_q2, lambda_k2, subln_g, w_out, g_norm2,
           w_ff1, w_ff2, g_final):
    B, S, D = x.shape
    T = B * S
    assert D == D_MODEL and S % TOKEN_TILE == 0 and S % Q_TILE == 0 and S % K_TILE == 0
    assert w_ada.shape[0] == 1, "single-layer block"
    tiles_per_batch = S // TOKEN_TILE
    row = lambda a: a.reshape(1, -1)

    mod, lam = _modulation(c, w_ada[0], b_ada[0], lambda_q1[0], lambda_k1[0],
                           lambda_q2[0], lambda_k2[0])
    mod3 = mod.reshape(B, 1, N_MOD * D)
    x2 = x.reshape(T, D)
    pos = positions.reshape(T, 1).astype(F32)
    bias_rows = jnp.broadcast_to(b_spatial[0][:, :, None], (GMLP_GROUPS, CHUNK, CHUNK))

    a, gb, q, k, v = _in_proj(pos, x2, mod3, row(g_norm1[0]), w_in[0].astype(BF16),
                              row(gmlp_ln_g[0]), row(gmlp_ln_b[0]),
                              w_spatial[0].astype(BF16), bias_rows, tiles_per_batch)
    bb = _attention(lam, q, k, v, row(subln_g[0]), B, S)
    out = _out_ffn(x2, a, gb, bb, mod3, w_out[0].astype(BF16), row(g_norm2[0]),
                   w_ff1[0].astype(BF16), w_ff2[0].astype(BF16), row(g_final),
                   tiles_per_batch)
    return out.reshape(B, S, D)
```

```python
import functools
import math

import jax
import jax.numpy as jnp
from jax import lax
from jax.experimental import pallas as pl
from jax.experimental.pallas import tpu as pltpu

D_MODEL = 1024
GMLP_GROUPS = 8
CHUNK = 128
DIFF_HEADS = 8
DIFF_VDIM = 128
DIFF_QK_DIM = 64
D_FF = 4 * D_MODEL
N_MOD = 6
N_IN_GROUPS = 7
ROPE_THETA = 10000.0
EPS = 1e-6
LAMBDA_INIT = 0.8 - 0.6 * math.exp(-0.3 * 0)

LANES = 128
TOKEN_TILE = 512
Q_TILE = 512
K_TILE = TOKEN_TILE
Q_BLOCK = 256
KT_UNROLL = 16
LOOKAHEAD = 4
SUM_ROWS = 16
LOG2E = math.log2(math.e)
VMEM_LIMIT = 48 * 1024 * 1024

F32 = jnp.float32
BF16 = jnp.bfloat16


def _rms(x, g):
    return x * lax.rsqrt(jnp.mean(x * x, axis=-1, keepdims=True) + EPS) * g


def _mod_kernel(ct_ref, w_ref, b_ref, lq1_ref, lk1_ref, lq2_ref, lk2_ref, mod_ref, lam_ref):
    ct = ct_ref[...]
    act = ct * jax.nn.sigmoid(ct)
    w = w_ref[...]
    for b in range(ct.shape[1]):
        row = jnp.sum(act[:, b:b + 1] * w, axis=0, keepdims=True)
        mod_ref[b:b + 1, :] = row + b_ref[...]

    @pl.when(pl.program_id(0) == 0)
    def _():
        s1 = jnp.sum(lq1_ref[...] * lk1_ref[...], axis=-1, keepdims=True)
        s2 = jnp.sum(lq2_ref[...] * lk2_ref[...], axis=-1, keepdims=True)
        lam = jnp.exp(s1) - jnp.exp(s2) + LAMBDA_INIT
        lam_ref[...] = jnp.broadcast_to(lam, lam_ref.shape)


def _modulation(c, w_ada, b_ada, lq1, lk1, lq2, lk2):
    B, D = c.shape
    N = w_ada.shape[1]
    tn = 512
    row = lambda a: a.reshape(1, -1)
    small = pl.BlockSpec((1, DIFF_QK_DIM), lambda n: (0, 0))
    return pl.pallas_call(
        _mod_kernel,
        grid=(N // tn,),
        in_specs=[
            pl.BlockSpec((D, B), lambda n: (0, 0)),
            pl.BlockSpec((D, tn), lambda n: (0, n)),
            pl.BlockSpec((1, tn), lambda n: (0, n)),
            small, small, small, small,
        ],
        out_specs=[
            pl.BlockSpec((B, tn), lambda n: (0, n)),
            pl.BlockSpec((1, LANES), lambda n: (0, 0)),
        ],
        out_shape=[
            jax.ShapeDtypeStruct((B, N), F32),
            jax.ShapeDtypeStruct((1, LANES), F32),
        ],
        compiler_params=pltpu.CompilerParams(
            dimension_semantics=("arbitrary",), vmem_limit_bytes=VMEM_LIMIT),
        name="adaln_mod",
    )(c.T, w_ada, row(b_ada), row(lq1), row(lk1), row(lq2), row(lk2))


def _rope(z, cos, sin_signed, lane):
    first_half = (lane % DIFF_QK_DIM) < (DIFF_QK_DIM // 2)
    partner = jnp.where(first_half,
                        pltpu.roll(z, LANES - DIFF_QK_DIM // 2, axis=1),
                        pltpu.roll(z, DIFF_QK_DIM // 2, axis=1))
    return z * cos + partner * sin_signed


def _in_proj_kernel(pos_ref, x_ref, mod_ref, g1_ref, w_ref, lng_ref, lnb_ref, ws_ref, bs_ref,
                    freq_ref, a_ref, gb_ref, q_ref, k_ref, vt_ref,
                    h_sc, u_sc, sv_sc, cos_sc, sin_sc):
    j = pl.program_id(1)
    tm = x_ref.shape[0]

    @pl.when(j == 0)
    def _():
        mod = mod_ref[0]
        sh1 = mod[:, 0:D_MODEL]
        sc1 = mod[:, D_MODEL:2 * D_MODEL]
        h = _rms(x_ref[...], g1_ref[...]) * (1.0 + sc1) + sh1
        h_sc[...] = h.astype(BF16)

    def proj():
        return jnp.dot(h_sc[...], w_ref[...], preferred_element_type=F32)

    @pl.when(j == 0)
    def _():
        u_sc[...] = jax.nn.gelu(proj())

    @pl.when(j == 1)
    def _():
        g = jax.nn.gelu(proj())
        mu = jnp.mean(g, axis=-1, keepdims=True)
        gc = g - mu
        var = jnp.mean(gc * gc, axis=-1, keepdims=True)
        vln = (gc * lax.rsqrt(var + EPS) * lng_ref[...] + lnb_ref[...]).astype(BF16)
        for grp in range(GMLP_GROUPS):
            cols = slice(grp * CHUNK, (grp + 1) * CHUNK)
            w_s = ws_ref[grp]
            b_s = bs_ref[grp]
            for n in range(tm // CHUNK):
                rows = slice(n * CHUNK, (n + 1) * CHUNK)
                sv_sc[rows, cols] = jnp.dot(w_s, vln[rows, cols],
                                            preferred_element_type=F32) + b_s

    @pl.when(j == 2)
    def _():
        ang = pos_ref[...] * freq_ref[...]
        lane = lax.broadcasted_iota(jnp.int32, (tm, LANES), 1)
        first_half = (lane % DIFF_QK_DIM) < (DIFF_QK_DIM // 2)
        sin = jnp.sin(ang)
        cos_sc[...] = jnp.cos(ang)
        sin_sc[...] = jnp.where(first_half, -sin, sin)

    def rope_store(out_ref, scale):
        z = proj()
        lane = lax.broadcasted_iota(jnp.int32, (tm, LANES), 1)
        cos = cos_sc[...]
        sin_signed = sin_sc[...]
        for s in range(D_MODEL // LANES):
            cols = slice(s * LANES, (s + 1) * LANES)
            r = _rope(z[:, cols], cos, sin_signed, lane)
            if scale != 1.0:
                r = r * scale
            out_ref[:, cols] = r.astype(out_ref.dtype)

    @pl.when(j == 2)
    def _():
        rope_store(q_ref, DIFF_QK_DIM ** -0.5 * LOG2E)

    @pl.when(j == 3)
    def _():
        rope_store(k_ref, 1.0)

    @pl.when(j == 4)
    def _():
        z = proj()
        for h in range(DIFF_HEADS):
            vt_ref[h] = z[:, h * DIFF_VDIM:(h + 1) * DIFF_VDIM].T.astype(vt_ref.dtype)

    @pl.when(j == 5)
    def _():
        a_ref[...] = jax.nn.sigmoid(proj()) * (u_sc[...] * sv_sc[...])

    @pl.when(j == 6)
    def _():
        gb_ref[...] = jax.nn.sigmoid(proj())


def _in_proj(pos, x2, mod3, g1, w_in, ln_g, ln_b, w_s, b_s, tiles_per_batch):
    T, D = x2.shape
    tm = TOKEN_TILE
    freq = ROPE_THETA ** (-jnp.arange(0, DIFF_QK_DIM, 2, dtype=F32) / DIFF_QK_DIM)
    freq = jnp.tile(freq, LANES // freq.shape[0]).reshape(1, LANES)
    const2 = lambda i, j: (0, 0)
    const3 = lambda i, j: (0, 0, 0)
    tok = lambda i, j: (i, 0)
    out_tok = pl.BlockSpec((tm, D), tok)
    return pl.pallas_call(
        _in_proj_kernel,
        grid=(T // tm, N_IN_GROUPS),
        in_specs=[
            pl.BlockSpec((tm, 1), tok),
            pl.BlockSpec((tm, D), tok),
            pl.BlockSpec((1, 1, N_MOD * D), lambda i, j: (i // tiles_per_batch, 0, 0)),
            pl.BlockSpec((1, D), const2),
            pl.BlockSpec((D, D), lambda i, j: (0, j)),
            pl.BlockSpec((1, D), const2),
            pl.BlockSpec((1, D), const2),
            pl.BlockSpec((GMLP_GROUPS, CHUNK, CHUNK), const3),
            pl.BlockSpec((GMLP_GROUPS, CHUNK, CHUNK), const3),
            pl.BlockSpec((1, LANES), const2),
        ],
        out_specs=[
            out_tok, out_tok, out_tok, out_tok,
            pl.BlockSpec((None, DIFF_HEADS, None, DIFF_VDIM, tm),
                         lambda i, j: (i // tiles_per_batch, 0, i % tiles_per_batch, 0, 0)),
        ],
        out_shape=[
            jax.ShapeDtypeStruct((T, D), F32),
            jax.ShapeDtypeStruct((T, D), F32),
            jax.ShapeDtypeStruct((T, D), BF16),
            jax.ShapeDtypeStruct((T, D), BF16),
            jax.ShapeDtypeStruct((T // (tm * tiles_per_batch), DIFF_HEADS, tiles_per_batch,
                                  DIFF_VDIM, tm), BF16),
        ],
        scratch_shapes=[
            pltpu.VMEM((tm, D), BF16),
            pltpu.VMEM((tm, D), F32),
            pltpu.VMEM((tm, D), F32),
            pltpu.VMEM((tm, LANES), F32),
            pltpu.VMEM((tm, LANES), F32),
        ],
        compiler_params=pltpu.CompilerParams(
            dimension_semantics=("arbitrary", "arbitrary"), vmem_limit_bytes=VMEM_LIMIT),
        name="in_proj",
    )(pos, x2, mod3, g1, w_in, ln_g, ln_b, w_s, b_s, freq)


def _attn_kernel(lam_ref, q_ref, k_ref, vt_ref, g_ref, o_ref, q12_sc, acc_sc):
    tq = q_ref.shape[0]
    n_kt, _, tk = vt_ref.shape
    n_blk = 2 * tq // Q_BLOCK

    q = q_ref[...]
    lane = lax.broadcasted_iota(jnp.int32, q.shape, 1)
    zero = jnp.zeros_like(q)
    q12_sc[0:tq, :] = jnp.where(lane < DIFF_QK_DIM, q, zero)
    q12_sc[tq:2 * tq, :] = jnp.where(lane >= DIFF_QK_DIM, q, zero)
    acc_sc[...] = jnp.zeros(acc_sc.shape, F32)

    ones_rows = jnp.ones((SUM_ROWS, tk), BF16)

    def body(grp, carry):
        def scores(u, c):
            start = pl.multiple_of((grp * KT_UNROLL + u) * tk, tk)
            rows = slice(c * Q_BLOCK, (c + 1) * Q_BLOCK)
            st = lax.dot_general(k_ref[pl.ds(start, tk), :], q12_sc[rows, :],
                                 (((1,), (1,)), ((), ())),
                                 preferred_element_type=F32)
            return st, jnp.max(st, axis=0, keepdims=True)

        blocks = [(u, c) for u in range(KT_UNROLL) for c in range(n_blk)]
        m_run = list(carry)
        pending = [scores(*blk) for blk in blocks[:LOOKAHEAD]]
        for idx, (u, c) in enumerate(blocks):
            rows = slice(c * Q_BLOCK, (c + 1) * Q_BLOCK)
            st, m_cur = pending.pop(0)
            if idx + LOOKAHEAD < len(blocks):
                pending.append(scores(*blocks[idx + LOOKAHEAD]))
            m_new = jnp.maximum(m_run[c], m_cur)
            alpha = jnp.exp2(m_run[c] - m_new)
            pt = jnp.exp2(st - m_new).astype(BF16)
            v_ext = jnp.concatenate([vt_ref[grp * KT_UNROLL + u], ones_rows], axis=0)
            acc_sc[:, rows] = alpha * acc_sc[:, rows] + jnp.dot(
                v_ext, pt, preferred_element_type=F32)
            m_run[c] = m_new
        return tuple(m_run)

    init = tuple(jnp.full((1, Q_BLOCK), -jnp.inf, F32) for _ in range(n_blk))
    lax.fori_loop(0, n_kt // KT_UNROLL, body, init)

    hd = o_ref.shape[1]
    inv_l = 1.0 / acc_sc[hd:hd + 1, :]
    acc = acc_sc[0:hd, :] * inv_l
    ot = acc[:, 0:tq] - lam_ref[0:1, 0:1] * acc[:, tq:2 * tq]
    ms = jnp.mean(ot * ot, axis=0, keepdims=True)
    ot = ot * lax.rsqrt(ms + EPS) * (g_ref[...] * (1.0 - LAMBDA_INIT))
    o_ref[...] = ot.T.astype(o_ref.dtype)


def _attention(lam, q, k, vt, subln_g, B, S):
    T, D = q.shape
    tq = Q_TILE
    nq = S // tq
    hd = DIFF_VDIM
    n_kt, tk = vt.shape[2], vt.shape[4]
    return pl.pallas_call(
        _attn_kernel,
        grid=(B, DIFF_HEADS, nq),
        in_specs=[
            pl.BlockSpec((1, LANES), lambda b, h, i: (0, 0)),
            pl.BlockSpec((tq, hd), lambda b, h, i: (b * nq + i, h)),
            pl.BlockSpec((S, hd), lambda b, h, i: (b, h)),
            pl.BlockSpec((None, None, n_kt, hd, tk),
                         lambda b, h, i: (b, h, 0, 0, 0)),
            pl.BlockSpec((hd, 1), lambda b, h, i: (0, 0)),
        ],
        out_specs=pl.BlockSpec((tq, hd), lambda b, h, i: (b * nq + i, h)),
        out_shape=jax.ShapeDtypeStruct((T, D), F32),
        scratch_shapes=[
            pltpu.VMEM((2 * tq, hd), BF16),
            pltpu.VMEM((hd + SUM_ROWS, 2 * tq), F32),
        ],
        compiler_params=pltpu.CompilerParams(
            dimension_semantics=("arbitrary", "arbitrary", "arbitrary"),
            vmem_limit_bytes=VMEM_LIMIT),
        name="diff_attn",
    )(lam, q, k, vt, subln_g)


def _out_ffn_kernel(x_ref, a_ref, gb_ref, bb_ref, mod_ref, wout_ref, g2_ref, w1_ref, w2_ref,
                    gf_ref, o_ref):
    mod = mod_ref[0]
    gt1 = mod[:, 2 * D_MODEL:3 * D_MODEL]
    sh2 = mod[:, 3 * D_MODEL:4 * D_MODEL]
    sc2 = mod[:, 4 * D_MODEL:5 * D_MODEL]
    gt2 = mod[:, 5 * D_MODEL:6 * D_MODEL]

    merged = a_ref[...] + gb_ref[...] * bb_ref[...]
    x1 = x_ref[...] + gt1 * jnp.dot(merged.astype(BF16), wout_ref[...],
                                    preferred_element_type=F32)
    h2 = (_rms(x1, g2_ref[...]) * (1.0 + sc2) + sh2).astype(BF16)
    ff = None
    for c in range(D_FF // D_MODEL):
        cols = slice(c * D_MODEL, (c + 1) * D_MODEL)
        hid = jnp.dot(h2, w1_ref[:, cols], preferred_element_type=F32)
        hid = jnp.square(jnp.maximum(hid, 0.0)).astype(BF16)
        part = jnp.dot(hid, w2_ref[cols, :], preferred_element_type=F32)
        ff = part if ff is None else ff + part
    x2 = x1 + gt2 * ff
    o_ref[...] = _rms(x2, gf_ref[...])


def _out_ffn(x2, a, gb, bb, mod3, w_out, g2, w1, w2, gf, tiles_per_batch):
    T, D = x2.shape
    tm = TOKEN_TILE
    tok = pl.BlockSpec((tm, D), lambda i: (i, 0))
    const = lambda i: (0, 0)
    resident = lambda shape: pl.BlockSpec(shape, const, pipeline_mode=pl.Buffered(1))
    return pl.pallas_call(
        _out_ffn_kernel,
        grid=(T // tm,),
        in_specs=[
            tok, tok, tok, tok,
            pl.BlockSpec((1, 1, N_MOD * D), lambda i: (i // tiles_per_batch, 0, 0)),
            resident((D, D)),
            pl.BlockSpec((1, D), const),
            resident((D, D_FF)),
            resident((D_FF, D)),
            pl.BlockSpec((1, D), const),
        ],
        out_specs=tok,
        out_shape=jax.ShapeDtypeStruct((T, D), F32),
        compiler_params=pltpu.CompilerParams(
            dimension_semantics=("arbitrary",), vmem_limit_bytes=VMEM_LIMIT),
        name="out_ffn",
    )(x2, a, gb, bb, mod3, w_out, g2, w1, w2, gf)


def kernel(x, c, positions, w_ada, b_ada, g_norm1, w_in, gmlp_ln_g, gmlp_ln_b, w_spatial,
           b_spatial, lambda_q1, lambda_k1, lambda_q2, lambda_k2, subln_g, w_out, g_norm2,
           w_ff1, w_ff2, g_final):
    B, S, D = x.shape
    T = B * S
    assert D == D_MODEL and S % TOKEN_TILE == 0 and S % Q_TILE == 0 and S % K_TILE == 0
    assert w_ada.shape[0] == 1, "single-layer block"
    tiles_per_batch = S // TOKEN_TILE
    row = lambda a: a.reshape(1, -1)

    mod, lam = _modulation(c, w_ada[0], b_ada[0], lambda_q1[0], lambda_k1[0],
                           lambda_q2[0], lambda_k2[0])
    mod3 = mod.reshape(B, 1, N_MOD * D)
    x2 = x.reshape(T, D)
    pos = positions.reshape(T, 1).astype(F32)
    bias_rows = jnp.broadcast_to(b_spatial[0][:, :, None], (GMLP_GROUPS, CHUNK, CHUNK))

    a, gb, q, k, vt = _in_proj(pos, x2, mod3, row(g_norm1[0]), w_in[0].astype(BF16),
                               row(gmlp_ln_g[0]), row(gmlp_ln_b[0]),
                               w_spatial[0].astype(BF16), bias_rows, tiles_per_batch)
    bb = _attention(lam, q, k, vt, subln_g[0].reshape(-1, 1), B, S)
    out = _out_ffn(x2, a, gb, bb, mod3, w_out[0].astype(BF16), row(g_norm2[0]),
                   w_ff1[0].astype(BF16), w_ff2[0].astype(BF16), row(g_final),
                   tiles_per_batch)
    return out.reshape(B, S, D)
```

```python
import math

import jax
import jax.numpy as jnp
from jax import lax
from jax.experimental import pallas as pl
from jax.experimental.pallas import tpu as pltpu

D_MODEL = 1024
GMLP_GROUPS = 8
CHUNK = 128
DIFF_HEADS = 8
DIFF_VDIM = 128
DIFF_QK_DIM = 64
D_FF = 4 * D_MODEL
N_MOD = 6
ROPE_THETA = 10000.0
EPS = 1e-6
LAMBDA_INIT = 0.8 - 0.6 * math.exp(-0.3 * 0)
COL_U, COL_V, COL_Q, COL_K, COL_VATT, COL_GATE_A, COL_GATE_B = range(7)

LANES = 128
TOKEN_TILE = 512
Q_TILE = 1024
K_TILE = TOKEN_TILE
Q_BLOCK = 256
LOOKAHEAD = 4
SUM_ROWS = 16
LOG2E = math.log2(math.e)
VMEM_LIMIT = 48 * 1024 * 1024
VMEM_LIMIT_RESIDENT = 56 * 1024 * 1024

F32 = jnp.float32
BF16 = jnp.bfloat16


def _rms(x, g):
    return x * lax.rsqrt(jnp.mean(x * x, axis=-1, keepdims=True) + EPS) * g


def _resident(shape):
    return pl.BlockSpec(shape, lambda *_: (0,) * len(shape), pipeline_mode=pl.Buffered(1))


def _mod_kernel(ct_ref, w_ref, b_ref, lq1_ref, lk1_ref, lq2_ref, lk2_ref, mod_ref, lam_ref):
    ct = ct_ref[...]
    act = ct * jax.nn.sigmoid(ct)
    w = w_ref[...]
    for b in range(ct.shape[1]):
        row = jnp.sum(act[:, b:b + 1] * w, axis=0, keepdims=True)
        mod_ref[b:b + 1, :] = row + b_ref[...]

    @pl.when(pl.program_id(0) == 0)
    def _():
        s1 = jnp.sum(lq1_ref[...] * lk1_ref[...], axis=-1, keepdims=True)
        s2 = jnp.sum(lq2_ref[...] * lk2_ref[...], axis=-1, keepdims=True)
        lam = jnp.exp(s1) - jnp.exp(s2) + LAMBDA_INIT
        lam_ref[...] = jnp.broadcast_to(lam, lam_ref.shape)


def _modulation(c, w_ada, b_ada, lq1, lk1, lq2, lk2):
    B, D = c.shape
    N = w_ada.shape[1]
    tn = 512
    row = lambda a: a.reshape(1, -1)
    small = pl.BlockSpec((1, DIFF_QK_DIM), lambda n: (0, 0))
    return pl.pallas_call(
        _mod_kernel,
        grid=(N // tn,),
        in_specs=[
            pl.BlockSpec((D, B), lambda n: (0, 0)),
            pl.BlockSpec((D, tn), lambda n: (0, n)),
            pl.BlockSpec((1, tn), lambda n: (0, n)),
            small, small, small, small,
        ],
        out_specs=[
            pl.BlockSpec((B, tn), lambda n: (0, n)),
            pl.BlockSpec((1, LANES), lambda n: (0, 0)),
        ],
        out_shape=[
            jax.ShapeDtypeStruct((B, N), F32),
            jax.ShapeDtypeStruct((1, LANES), F32),
        ],
        compiler_params=pltpu.CompilerParams(
            dimension_semantics=("arbitrary",), vmem_limit_bytes=VMEM_LIMIT),
        name="adaln_mod",
    )(c.T, w_ada, row(b_ada), row(lq1), row(lk1), row(lq2), row(lk2))


def _rope(z, cos, sin_signed, first_half):
    partner = jnp.where(first_half,
                        pltpu.roll(z, LANES - DIFF_QK_DIM // 2, axis=1),
                        pltpu.roll(z, DIFF_QK_DIM // 2, axis=1))
    return z * cos + partner * sin_signed


def _in_proj_kernel(pos_ref, x_ref, mod_ref, g1_ref, w_ref, lng_ref, lnb_ref, ws_ref, bs_ref,
                    freq_ref, a_ref, gb_ref, q_ref, k_ref, vt_ref, h_sc, u_sc, sv_sc):
    tm = x_ref.shape[0]
    mod = mod_ref[0]
    sh1 = mod[:, 0:D_MODEL]
    sc1 = mod[:, D_MODEL:2 * D_MODEL]
    h_sc[...] = (_rms(x_ref[...], g1_ref[...]) * (1.0 + sc1) + sh1).astype(BF16)

    def proj(group):
        cols = slice(group * D_MODEL, (group + 1) * D_MODEL)
        return jnp.dot(h_sc[...], w_ref[:, cols], preferred_element_type=F32)

    def gmlp_u(z):
        u_sc[...] = jax.nn.gelu(z)

    def gmlp_v(z):
        g = jax.nn.gelu(z)
        gc = g - jnp.mean(g, axis=-1, keepdims=True)
        var = jnp.mean(gc * gc, axis=-1, keepdims=True)
        vln = (gc * lax.rsqrt(var + EPS) * lng_ref[...] + lnb_ref[...]).astype(BF16)
        for grp in range(GMLP_GROUPS):
            cols = slice(grp * CHUNK, (grp + 1) * CHUNK)
            w_s = ws_ref[grp]
            b_s = bs_ref[grp]
            for n in range(tm // CHUNK):
                rows = slice(n * CHUNK, (n + 1) * CHUNK)
                sv_sc[rows, cols] = jnp.dot(w_s, vln[rows, cols],
                                            preferred_element_type=F32) + b_s

    def gate_a(z):
        a_ref[...] = jax.nn.sigmoid(z) * (u_sc[...] * sv_sc[...])

    ang = pos_ref[...] * freq_ref[...]
    lane = lax.broadcasted_iota(jnp.int32, (tm, LANES), 1)
    first_half = (lane % DIFF_QK_DIM) < (DIFF_QK_DIM // 2)
    cos = jnp.cos(ang)
    sin = jnp.sin(ang)
    sin_signed = jnp.where(first_half, -sin, sin)

    def rope_store(out_ref, scale):
        def store(z):
            for s in range(D_MODEL // LANES):
                cols = slice(s * LANES, (s + 1) * LANES)
                r = _rope(z[:, cols], cos, sin_signed, first_half)
                if scale != 1.0:
                    r = r * scale
                out_ref[:, cols] = r.astype(out_ref.dtype)
        return store

    def v_attn(z):
        for h in range(DIFF_HEADS):
            vt_ref[h] = z[:, h * DIFF_VDIM:(h + 1) * DIFF_VDIM].T.astype(vt_ref.dtype)

    def gate_b(z):
        gb_ref[...] = jax.nn.sigmoid(z)

    stages = [
        (COL_U, gmlp_u), (COL_V, gmlp_v), (COL_GATE_A, gate_a),
        (COL_Q, rope_store(q_ref, DIFF_QK_DIM ** -0.5 * LOG2E)), (COL_K, rope_store(k_ref, 1.0)),
        (COL_VATT, v_attn), (COL_GATE_B, gate_b),
    ]
    z_next = proj(stages[0][0])
    for idx, (_, epilogue) in enumerate(stages):
        z = z_next
        if idx + 1 < len(stages):
            z_next = proj(stages[idx + 1][0])
        epilogue(z)


def _in_proj(pos, x2, mod3, g1, w_in, ln_g, ln_b, w_s, b_s, tiles_per_batch):
    T, D = x2.shape
    tm = TOKEN_TILE
    freq = ROPE_THETA ** (-jnp.arange(0, DIFF_QK_DIM, 2, dtype=F32) / DIFF_QK_DIM)
    freq = jnp.tile(freq, LANES // freq.shape[0]).reshape(1, LANES)
    tok = lambda i: (i, 0)
    out_tok = pl.BlockSpec((tm, D), tok)
    return pl.pallas_call(
        _in_proj_kernel,
        grid=(T // tm,),
        in_specs=[
            pl.BlockSpec((tm, 1), tok),
            pl.BlockSpec((tm, D), tok),
            pl.BlockSpec((1, 1, N_MOD * D), lambda i: (i // tiles_per_batch, 0, 0)),
            _resident((1, D)),
            _resident(w_in.shape),
            _resident((1, D)),
            _resident((1, D)),
            _resident((GMLP_GROUPS, CHUNK, CHUNK)),
            _resident((GMLP_GROUPS, CHUNK, CHUNK)),
            _resident((1, LANES)),
        ],
        out_specs=[
            out_tok, out_tok, out_tok, out_tok,
            pl.BlockSpec((None, DIFF_HEADS, None, DIFF_VDIM, tm),
                         lambda i: (i // tiles_per_batch, 0, i % tiles_per_batch, 0, 0)),
        ],
        out_shape=[
            jax.ShapeDtypeStruct((T, D), F32),
            jax.ShapeDtypeStruct((T, D), F32),
            jax.ShapeDtypeStruct((T, D), BF16),
            jax.ShapeDtypeStruct((T, D), BF16),
            jax.ShapeDtypeStruct((T // (tm * tiles_per_batch), DIFF_HEADS, tiles_per_batch,
                                  DIFF_VDIM, tm), BF16),
        ],
        scratch_shapes=[
            pltpu.VMEM((tm, D), BF16),
            pltpu.VMEM((tm, D), F32),
            pltpu.VMEM((tm, D), F32),
        ],
        compiler_params=pltpu.CompilerParams(
            dimension_semantics=("arbitrary",), vmem_limit_bytes=VMEM_LIMIT_RESIDENT),
        name="in_proj",
    )(pos, x2, mod3, g1, w_in, ln_g, ln_b, w_s, b_s, freq)


def _attn_kernel(lam_ref, q_ref, k_ref, vt_ref, g_ref, o_ref, q12_sc, acc_sc):
    tq, hd = q_ref.shape
    n_kt, _, tk = vt_ref.shape
    n_blk = 2 * tq // Q_BLOCK

    q = q_ref[...]
    lane = lax.broadcasted_iota(jnp.int32, q.shape, 1)
    zero = jnp.zeros_like(q)
    q12_sc[0:tq, :] = jnp.where(lane < DIFF_QK_DIM, q, zero)
    q12_sc[tq:2 * tq, :] = jnp.where(lane >= DIFF_QK_DIM, q, zero)
    acc_sc[...] = jnp.zeros(acc_sc.shape, F32)
    ones_rows = jnp.ones((SUM_ROWS, tk), BF16)

    def scores(kt, c):
        rows = slice(c * Q_BLOCK, (c + 1) * Q_BLOCK)
        st = lax.dot_general(k_ref[kt * tk:(kt + 1) * tk, :], q12_sc[rows, :],
                             (((1,), (1,)), ((), ())),
                             preferred_element_type=F32)
        return st, jnp.max(st, axis=0, keepdims=True)

    blocks = [(kt, c) for kt in range(n_kt) for c in range(n_blk)]
    m_run = [jnp.full((1, Q_BLOCK), -jnp.inf, F32) for _ in range(n_blk)]
    pending = [scores(*blk) for blk in blocks[:LOOKAHEAD]]
    for idx, (kt, c) in enumerate(blocks):
        rows = slice(c * Q_BLOCK, (c + 1) * Q_BLOCK)
        st, m_cur = pending.pop(0)
        if idx + LOOKAHEAD < len(blocks):
            pending.append(scores(*blocks[idx + LOOKAHEAD]))
        m_new = jnp.maximum(m_run[c], m_cur)
        alpha = jnp.exp2(m_run[c] - m_new)
        pt = jnp.exp2(st - m_new).astype(BF16)
        v_ext = jnp.concatenate([vt_ref[kt], ones_rows], axis=0)
        acc_sc[:, rows] = alpha * acc_sc[:, rows] + jnp.dot(
            v_ext, pt, preferred_element_type=F32)
        m_run[c] = m_new

    inv_l = 1.0 / acc_sc[hd:hd + 1, :]
    acc = acc_sc[0:hd, :] * inv_l
    ot = acc[:, 0:tq] - lam_ref[0:1, 0:1] * acc[:, tq:2 * tq]
    ms = jnp.mean(ot * ot, axis=0, keepdims=True)
    ot = ot * lax.rsqrt(ms + EPS) * (g_ref[...] * (1.0 - LAMBDA_INIT))
    o_ref[...] = ot.T.astype(o_ref.dtype)


def _attention(lam, q, k, vt, subln_g, B, S):
    T, D = q.shape
    tq = Q_TILE
    nq = S // tq
    hd = DIFF_VDIM
    n_kt, tk = vt.shape[2], vt.shape[4]
    return pl.pallas_call(
        _attn_kernel,
        grid=(B, DIFF_HEADS, nq),
        in_specs=[
            pl.BlockSpec((1, LANES), lambda b, h, i: (0, 0)),
            pl.BlockSpec((tq, hd), lambda b, h, i: (b * nq + i, h)),
            pl.BlockSpec((S, hd), lambda b, h, i: (b, h)),
            pl.BlockSpec((None, None, n_kt, hd, tk),
                         lambda b, h, i: (b, h, 0, 0, 0)),
            pl.BlockSpec((hd, 1), lambda b, h, i: (0, 0)),
        ],
        out_specs=pl.BlockSpec((tq, hd), lambda b, h, i: (b * nq + i, h)),
        out_shape=jax.ShapeDtypeStruct((T, D), F32),
        scratch_shapes=[
            pltpu.VMEM((2 * tq, hd), BF16),
            pltpu.VMEM((hd + SUM_ROWS, 2 * tq), F32),
        ],
        compiler_params=pltpu.CompilerParams(
            dimension_semantics=("arbitrary", "arbitrary", "arbitrary"),
            vmem_limit_bytes=VMEM_LIMIT),
        name="diff_attn",
    )(lam, q, k, vt, subln_g)


def _out_ffn_kernel(x_ref, a_ref, gb_ref, bb_ref, mod_ref, wout_ref, g2_ref, w1_ref, w2_ref,
                    gf_ref, o_ref):
    mod = mod_ref[0]
    gt1 = mod[:, 2 * D_MODEL:3 * D_MODEL]
    sh2 = mod[:, 3 * D_MODEL:4 * D_MODEL]
    sc2 = mod[:, 4 * D_MODEL:5 * D_MODEL]
    gt2 = mod[:, 5 * D_MODEL:6 * D_MODEL]

    merged = a_ref[...] + gb_ref[...] * bb_ref[...]
    x1 = x_ref[...] + gt1 * jnp.dot(merged.astype(BF16), wout_ref[...],
                                    preferred_element_type=F32)
    h2 = (_rms(x1, g2_ref[...]) * (1.0 + sc2) + sh2).astype(BF16)
    ff = None
    for c in range(D_FF // D_MODEL):
        cols = slice(c * D_MODEL, (c + 1) * D_MODEL)
        hid = jnp.dot(h2, w1_ref[:, cols], preferred_element_type=F32)
        hid = jnp.square(jnp.maximum(hid, 0.0)).astype(BF16)
        part = jnp.dot(hid, w2_ref[cols, :], preferred_element_type=F32)
        ff = part if ff is None else ff + part
    x2 = x1 + gt2 * ff
    o_ref[...] = _rms(x2, gf_ref[...])


def _out_ffn(x2, a, gb, bb, mod3, w_out, g2, w1, w2, gf, tiles_per_batch):
    T, D = x2.shape
    tm = TOKEN_TILE
    tok = pl.BlockSpec((tm, D), lambda i: (i, 0))
    return pl.pallas_call(
        _out_ffn_kernel,
        grid=(T // tm,),
        in_specs=[
            tok, tok, tok, tok,
            pl.BlockSpec((1, 1, N_MOD * D), lambda i: (i // tiles_per_batch, 0, 0)),
            _resident((D, D)),
            _resident((1, D)),
            _resident((D, D_FF)),
            _resident((D_FF, D)),
            _resident((1, D)),
        ],
        out_specs=tok,
        out_shape=jax.ShapeDtypeStruct((T, D), F32),
        compiler_params=pltpu.CompilerParams(
            dimension_semantics=("arbitrary",), vmem_limit_bytes=VMEM_LIMIT),
        name="out_ffn",
    )(x2, a, gb, bb, mod3, w_out, g2, w1, w2, gf)


def kernel(x, c, positions, w_ada, b_ada, g_norm1, w_in, gmlp_ln_g, gmlp_ln_b, w_spatial,
           b_spatial, lambda_q1, lambda_k1, lambda_q2, lambda_k2, subln_g, w_out, g_norm2,
           w_ff1, w_ff2, g_final):
    B, S, D = x.shape
    T = B * S
    assert D == D_MODEL and S % TOKEN_TILE == 0 and S % Q_TILE == 0
    assert w_ada.shape[0] == 1, "single-layer block"
    tiles_per_batch = S // TOKEN_TILE
    row = lambda a: a.reshape(1, -1)

    mod, lam = _modulation(c, w_ada[0], b_ada[0], lambda_q1[0], lambda_k1[0],
                           lambda_q2[0], lambda_k2[0])
    mod3 = mod.reshape(B, 1, N_MOD * D)
    x2 = x.reshape(T, D)
    pos = positions.reshape(T, 1).astype(F32)
    bias_rows = jnp.broadcast_to(b_spatial[0][:, :, None], (GMLP_GROUPS, CHUNK, CHUNK))

    a, gb, q, k, vt = _in_proj(pos, x2, mod3, row(g_norm1[0]), w_in[0].astype(BF16),
                               row(gmlp_ln_g[0]), row(gmlp_ln_b[0]),
                               w_spatial[0].astype(BF16), bias_rows, tiles_per_batch)
    bb = _attention(lam, q, k, vt, subln_g[0].reshape(-1, 1), B, S)
    out = _out_ffn(x2, a, gb, bb, mod3, w_out[0].astype(BF16), row(g_norm2[0]),
                   w_ff1[0].astype(BF16), w_ff2[0].astype(BF16), row(g_final),
                   tiles_per_batch)
    return out.reshape(B, S, D)
```

```python
import math

import jax
import jax.numpy as jnp
from jax import lax
from jax.experimental import pallas as pl
from jax.experimental.pallas import tpu as pltpu

D_MODEL = 1024
GMLP_GROUPS = 8
CHUNK = 128
DIFF_HEADS = 8
DIFF_VDIM = 128
DIFF_QK_DIM = 64
D_FF = 4 * D_MODEL
N_MOD = 6
ROPE_THETA = 10000.0
EPS = 1e-6
LAMBDA_INIT = 0.8 - 0.6 * math.exp(-0.3 * 0)
COL_U, COL_V, COL_Q, COL_K, COL_VATT, COL_GATE_A, COL_GATE_B = range(7)

LANES = 128
TOKEN_TILE = 512
Q_TILE = 1024
K_TILE = TOKEN_TILE
Q_BLOCK = 256
LOOKAHEAD = 4
SUM_ROWS = 16
LOG2E = math.log2(math.e)
VMEM_LIMIT = 48 * 1024 * 1024
VMEM_LIMIT_RESIDENT = 56 * 1024 * 1024

F32 = jnp.float32
BF16 = jnp.bfloat16


def _rms(x, g):
    return x * lax.rsqrt(jnp.mean(x * x, axis=-1, keepdims=True) + EPS) * g


def _resident(shape):
    return pl.BlockSpec(shape, lambda *_: (0,) * len(shape), pipeline_mode=pl.Buffered(1))


def _mod_kernel(ct_ref, w_ref, b_ref, lq1_ref, lk1_ref, lq2_ref, lk2_ref, mod_ref, lam_ref):
    ct = ct_ref[...]
    act = ct * jax.nn.sigmoid(ct)
    w = w_ref[...]
    for b in range(ct.shape[1]):
        row = jnp.sum(act[:, b:b + 1] * w, axis=0, keepdims=True)
        mod_ref[b:b + 1, :] = row + b_ref[...]

    @pl.when(pl.program_id(0) == 0)
    def _():
        s1 = jnp.sum(lq1_ref[...] * lk1_ref[...], axis=-1, keepdims=True)
        s2 = jnp.sum(lq2_ref[...] * lk2_ref[...], axis=-1, keepdims=True)
        lam = jnp.exp(s1) - jnp.exp(s2) + LAMBDA_INIT
        lam_ref[...] = jnp.broadcast_to(lam, lam_ref.shape)


def _modulation(c, w_ada, b_ada, lq1, lk1, lq2, lk2):
    B, D = c.shape
    N = w_ada.shape[1]
    tn = 512
    row = lambda a: a.reshape(1, -1)
    small = pl.BlockSpec((1, DIFF_QK_DIM), lambda n: (0, 0))
    return pl.pallas_call(
        _mod_kernel,
        grid=(N // tn,),
        in_specs=[
            pl.BlockSpec((D, B), lambda n: (0, 0)),
            pl.BlockSpec((D, tn), lambda n: (0, n)),
            pl.BlockSpec((1, tn), lambda n: (0, n)),
            small, small, small, small,
        ],
        out_specs=[
            pl.BlockSpec((B, tn), lambda n: (0, n)),
            pl.BlockSpec((1, LANES), lambda n: (0, 0)),
        ],
        out_shape=[
            jax.ShapeDtypeStruct((B, N), F32),
            jax.ShapeDtypeStruct((1, LANES), F32),
        ],
        compiler_params=pltpu.CompilerParams(
            dimension_semantics=("arbitrary",), vmem_limit_bytes=VMEM_LIMIT),
        name="adaln_mod",
    )(c.T, w_ada, row(b_ada), row(lq1), row(lk1), row(lq2), row(lk2))


def _rope(z, cos, sin_signed, first_half):
    partner = jnp.where(first_half,
                        pltpu.roll(z, LANES - DIFF_QK_DIM // 2, axis=1),
                        pltpu.roll(z, DIFF_QK_DIM // 2, axis=1))
    return z * cos + partner * sin_signed


def _in_proj_kernel(pos_ref, x_ref, mod_ref, g1_ref, w_ref, lng_ref, lnb_ref, ws_ref, bs_ref,
                    freq_ref, a_ref, gb_ref, q_ref, k_ref, vt_ref, h_sc, u_sc, sv_sc):
    tm = x_ref.shape[0]
    mod = mod_ref[0]
    sh1 = mod[:, 0:D_MODEL]
    sc1 = mod[:, D_MODEL:2 * D_MODEL]
    h_sc[...] = (_rms(x_ref[...], g1_ref[...]) * (1.0 + sc1) + sh1).astype(BF16)

    def proj(group):
        cols = slice(group * D_MODEL, (group + 1) * D_MODEL)
        return jnp.dot(h_sc[...], w_ref[:, cols], preferred_element_type=F32)

    def gmlp_u(z):
        u_sc[...] = jax.nn.gelu(z)

    def gmlp_v(z):
        g = jax.nn.gelu(z)
        gc = g - jnp.mean(g, axis=-1, keepdims=True)
        var = jnp.mean(gc * gc, axis=-1, keepdims=True)
        vln = (gc * lax.rsqrt(var + EPS) * lng_ref[...] + lnb_ref[...]).astype(BF16)
        for grp in range(GMLP_GROUPS):
            cols = slice(grp * CHUNK, (grp + 1) * CHUNK)
            w_s = ws_ref[grp]
            b_s = bs_ref[grp]
            for n in range(tm // CHUNK):
                rows = slice(n * CHUNK, (n + 1) * CHUNK)
                sv_sc[rows, cols] = jnp.dot(w_s, vln[rows, cols],
                                            preferred_element_type=F32) + b_s

    def gate_a(z):
        a_ref[...] = jax.nn.sigmoid(z) * (u_sc[...] * sv_sc[...])

    ang_t = freq_ref[...] * pos_ref[...]
    dim_t = lax.broadcasted_iota(jnp.int32, (LANES, tm), 0)
    sin_t = jnp.sin(ang_t)
    cos = jnp.cos(ang_t).T
    sin_signed = jnp.where((dim_t % DIFF_QK_DIM) < (DIFF_QK_DIM // 2), -sin_t, sin_t).T
    lane = lax.broadcasted_iota(jnp.int32, (tm, LANES), 1)
    first_half = (lane % DIFF_QK_DIM) < (DIFF_QK_DIM // 2)

    def rope_store(out_ref, scale):
        def store(z):
            for s in range(D_MODEL // LANES):
                cols = slice(s * LANES, (s + 1) * LANES)
                r = _rope(z[:, cols], cos, sin_signed, first_half)
                if scale != 1.0:
                    r = r * scale
                out_ref[:, cols] = r.astype(out_ref.dtype)
        return store

    def v_attn(z):
        for h in range(DIFF_HEADS):
            vt_ref[h] = z[:, h * DIFF_VDIM:(h + 1) * DIFF_VDIM].T.astype(vt_ref.dtype)

    def gate_b(z):
        gb_ref[...] = jax.nn.sigmoid(z)

    stages = [
        (COL_U, gmlp_u), (COL_V, gmlp_v), (COL_GATE_A, gate_a),
        (COL_Q, rope_store(q_ref, DIFF_QK_DIM ** -0.5 * LOG2E)), (COL_K, rope_store(k_ref, 1.0)),
        (COL_VATT, v_attn), (COL_GATE_B, gate_b),
    ]
    z_next = proj(stages[0][0])
    for idx, (_, epilogue) in enumerate(stages):
        z = z_next
        if idx + 1 < len(stages):
            z_next = proj(stages[idx + 1][0])
        epilogue(z)


def _in_proj(pos, x2, mod3, g1, w_in, ln_g, ln_b, w_s, b_s, tiles_per_batch):
    T, D = x2.shape
    tm = TOKEN_TILE
    freq = ROPE_THETA ** (-jnp.arange(0, DIFF_QK_DIM, 2, dtype=F32) / DIFF_QK_DIM)
    freq = jnp.tile(freq, LANES // freq.shape[0]).reshape(LANES, 1)
    tok = lambda i: (i, 0)
    out_tok = pl.BlockSpec((tm, D), tok)
    return pl.pallas_call(
        _in_proj_kernel,
        grid=(T // tm,),
        in_specs=[
            pl.BlockSpec((None, 1, tm), lambda i: (i, 0, 0)),
            pl.BlockSpec((tm, D), tok),
            pl.BlockSpec((1, 1, N_MOD * D), lambda i: (i // tiles_per_batch, 0, 0)),
            _resident((1, D)),
            _resident(w_in.shape),
            _resident((1, D)),
            _resident((1, D)),
            _resident((GMLP_GROUPS, CHUNK, CHUNK)),
            _resident((GMLP_GROUPS, CHUNK, CHUNK)),
            _resident((LANES, 1)),
        ],
        out_specs=[
            out_tok, out_tok, out_tok, out_tok,
            pl.BlockSpec((None, DIFF_HEADS, None, DIFF_VDIM, tm),
                         lambda i: (i // tiles_per_batch, 0, i % tiles_per_batch, 0, 0)),
        ],
        out_shape=[
            jax.ShapeDtypeStruct((T, D), F32),
            jax.ShapeDtypeStruct((T, D), F32),
            jax.ShapeDtypeStruct((T, D), BF16),
            jax.ShapeDtypeStruct((T, D), BF16),
            jax.ShapeDtypeStruct((T // (tm * tiles_per_batch), DIFF_HEADS, tiles_per_batch,
                                  DIFF_VDIM, tm), BF16),
        ],
        scratch_shapes=[
            pltpu.VMEM((tm, D), BF16),
            pltpu.VMEM((tm, D), F32),
            pltpu.VMEM((tm, D), F32),
        ],
        compiler_params=pltpu.CompilerParams(
            dimension_semantics=("arbitrary",), vmem_limit_bytes=VMEM_LIMIT_RESIDENT),
        name="in_proj",
    )(pos, x2, mod3, g1, w_in, ln_g, ln_b, w_s, b_s, freq)


def _attn_kernel(lam_ref, q_ref, k_ref, vt_ref, g_ref, o_ref, q12_sc, acc_sc):
    tq, hd = q_ref.shape
    n_kt, _, tk = vt_ref.shape
    n_blk = 2 * tq // Q_BLOCK

    q = q_ref[...]
    lane = lax.broadcasted_iota(jnp.int32, q.shape, 1)
    zero = jnp.zeros_like(q)
    q12_sc[0:tq, :] = jnp.where(lane < DIFF_QK_DIM, q, zero)
    q12_sc[tq:2 * tq, :] = jnp.where(lane >= DIFF_QK_DIM, q, zero)
    acc_sc[...] = jnp.zeros(acc_sc.shape, F32)
    ones_rows = jnp.ones((SUM_ROWS, tk), BF16)

    def scores(kt, c):
        rows = slice(c * Q_BLOCK, (c + 1) * Q_BLOCK)
        st = lax.dot_general(k_ref[kt * tk:(kt + 1) * tk, :], q12_sc[rows, :],
                             (((1,), (1,)), ((), ())),
                             preferred_element_type=F32)
        return st, jnp.max(st, axis=0, keepdims=True)

    blocks = [(kt, c) for kt in range(n_kt) for c in range(n_blk)]
    m_run = [jnp.full((1, Q_BLOCK), -jnp.inf, F32) for _ in range(n_blk)]
    pending = [scores(*blk) for blk in blocks[:LOOKAHEAD]]
    for idx, (kt, c) in enumerate(blocks):
        rows = slice(c * Q_BLOCK, (c + 1) * Q_BLOCK)
        st, m_cur = pending.pop(0)
        if idx + LOOKAHEAD < len(blocks):
            pending.append(scores(*blocks[idx + LOOKAHEAD]))
        m_new = jnp.maximum(m_run[c], m_cur)
        alpha = jnp.exp2(m_run[c] - m_new)
        pt = jnp.exp2(st - m_new).astype(BF16)
        v_ext = jnp.concatenate([vt_ref[kt], ones_rows], axis=0)
        acc_sc[:, rows] = alpha * acc_sc[:, rows] + jnp.dot(
            v_ext, pt, preferred_element_type=F32)
        m_run[c] = m_new

    inv_l = 1.0 / acc_sc[hd:hd + 1, :]
    acc = acc_sc[0:hd, :] * inv_l
    ot = acc[:, 0:tq] - lam_ref[0:1, 0:1] * acc[:, tq:2 * tq]
    ms = jnp.mean(ot * ot, axis=0, keepdims=True)
    ot = ot * lax.rsqrt(ms + EPS) * (g_ref[...] * (1.0 - LAMBDA_INIT))
    o_ref[...] = ot.T.astype(o_ref.dtype)


def _attention(lam, q, k, vt, subln_g, B, S):
    T, D = q.shape
    tq = Q_TILE
    nq = S // tq
    hd = DIFF_VDIM
    n_kt, tk = vt.shape[2], vt.shape[4]
    return pl.pallas_call(
        _attn_kernel,
        grid=(B, DIFF_HEADS, nq),
        in_specs=[
            pl.BlockSpec((1, LANES), lambda b, h, i: (0, 0)),
            pl.BlockSpec((tq, hd), lambda b, h, i: (b * nq + i, h)),
            pl.BlockSpec((S, hd), lambda b, h, i: (b, h)),
            pl.BlockSpec((None, None, n_kt, hd, tk),
                         lambda b, h, i: (b, h, 0, 0, 0)),
            pl.BlockSpec((hd, 1), lambda b, h, i: (0, 0)),
        ],
        out_specs=pl.BlockSpec((tq, hd), lambda b, h, i: (b * nq + i, h)),
        out_shape=jax.ShapeDtypeStruct((T, D), F32),
        scratch_shapes=[
            pltpu.VMEM((2 * tq, hd), BF16),
            pltpu.VMEM((hd + SUM_ROWS, 2 * tq), F32),
        ],
        compiler_params=pltpu.CompilerParams(
            dimension_semantics=("arbitrary", "arbitrary", "arbitrary"),
            vmem_limit_bytes=VMEM_LIMIT),
        name="diff_attn",
    )(lam, q, k, vt, subln_g)


def _out_ffn_kernel(x_ref, a_ref, gb_ref, bb_ref, mod_ref, wout_ref, g2_ref, w1_ref, w2_ref,
                    gf_ref, o_ref):
    mod = mod_ref[0]
    gt1 = mod[:, 2 * D_MODEL:3 * D_MODEL]
    sh2 = mod[:, 3 * D_MODEL:4 * D_MODEL]
    sc2 = mod[:, 4 * D_MODEL:5 * D_MODEL]
    gt2 = mod[:, 5 * D_MODEL:6 * D_MODEL]

    merged = a_ref[...] + gb_ref[...] * bb_ref[...]
    x1 = x_ref[...] + gt1 * jnp.dot(merged.astype(BF16), wout_ref[...],
                                    preferred_element_type=F32)
    h2 = (_rms(x1, g2_ref[...]) * (1.0 + sc2) + sh2).astype(BF16)
    ff = None
    for c in range(D_FF // D_MODEL):
        cols = slice(c * D_MODEL, (c + 1) * D_MODEL)
        hid = jnp.dot(h2, w1_ref[:, cols], preferred_element_type=F32)
        hid = jnp.square(jnp.maximum(hid, 0.0)).astype(BF16)
        part = jnp.dot(hid, w2_ref[cols, :], preferred_element_type=F32)
        ff = part if ff is None else ff + part
    x2 = x1 + gt2 * ff
    o_ref[...] = _rms(x2, gf_ref[...])


def _out_ffn(x2, a, gb, bb, mod3, w_out, g2, w1, w2, gf, tiles_per_batch):
    T, D = x2.shape
    tm = TOKEN_TILE
    tok = pl.BlockSpec((tm, D), lambda i: (i, 0))
    return pl.pallas_call(
        _out_ffn_kernel,
        grid=(T // tm,),
        in_specs=[
            tok, tok, tok, tok,
            pl.BlockSpec((1, 1, N_MOD * D), lambda i: (i // tiles_per_batch, 0, 0)),
            _resident((D, D)),
            _resident((1, D)),
            _resident((D, D_FF)),
            _resident((D_FF, D)),
            _resident((1, D)),
        ],
        out_specs=tok,
        out_shape=jax.ShapeDtypeStruct((T, D), F32),
        compiler_params=pltpu.CompilerParams(
            dimension_semantics=("arbitrary",), vmem_limit_bytes=VMEM_LIMIT),
        name="out_ffn",
    )(x2, a, gb, bb, mod3, w_out, g2, w1, w2, gf)


def kernel(x, c, positions, w_ada, b_ada, g_norm1, w_in, gmlp_ln_g, gmlp_ln_b, w_spatial,
           b_spatial, lambda_q1, lambda_k1, lambda_q2, lambda_k2, subln_g, w_out, g_norm2,
           w_ff1, w_ff2, g_final):
    B, S, D = x.shape
    T = B * S
    assert D == D_MODEL and S % TOKEN_TILE == 0 and S % Q_TILE == 0
    assert w_ada.shape[0] == 1, "single-layer block"
    tiles_per_batch = S // TOKEN_TILE
    row = lambda a: a.reshape(1, -1)

    mod, lam = _modulation(c, w_ada[0], b_ada[0], lambda_q1[0], lambda_k1[0],
                           lambda_q2[0], lambda_k2[0])
    mod3 = mod.reshape(B, 1, N_MOD * D)
    x2 = x.reshape(T, D)
    pos = positions.reshape(T // TOKEN_TILE, 1, TOKEN_TILE).astype(F32)
    bias_rows = jnp.broadcast_to(b_spatial[0][:, :, None], (GMLP_GROUPS, CHUNK, CHUNK))

    a, gb, q, k, vt = _in_proj(pos, x2, mod3, row(g_norm1[0]), w_in[0].astype(BF16),
                               row(gmlp_ln_g[0]), row(gmlp_ln_b[0]),
                               w_spatial[0].astype(BF16), bias_rows, tiles_per_batch)
    bb = _attention(lam, q, k, vt, subln_g[0].reshape(-1, 1), B, S)
    out = _out_ffn(x2, a, gb, bb, mod3, w_out[0].astype(BF16), row(g_norm2[0]),
                   w_ff1[0].astype(BF16), w_ff2[0].astype(BF16), row(g_final),
                   tiles_per_batch)
    return out.reshape(B, S, D)
```

```python
import math

import jax
import jax.numpy as jnp
from jax import lax
from jax.experimental import pallas as pl
from jax.experimental.pallas import tpu as pltpu

D_MODEL = 1024
GMLP_GROUPS = 8
CHUNK = 128
DIFF_HEADS = 8
DIFF_VDIM = 128
DIFF_QK_DIM = 64
D_FF = 4 * D_MODEL
N_MOD = 6
ROPE_THETA = 10000.0
EPS = 1e-6
LAMBDA_INIT = 0.8 - 0.6 * math.exp(-0.3 * 0)
COL_U, COL_V, COL_Q, COL_K, COL_VATT, COL_GATE_A, COL_GATE_B = range(7)

LANES = 128
TOKEN_TILE = 512
Q_TILE = 2048
K_TILE = TOKEN_TILE
Q_BLOCK = 256
LOOKAHEAD = 4
SUM_ROWS = 16
LOG2E = math.log2(math.e)
VMEM_LIMIT = 48 * 1024 * 1024
VMEM_LIMIT_RESIDENT = 56 * 1024 * 1024

F32 = jnp.float32
BF16 = jnp.bfloat16


def _rms(x, g):
    return x * lax.rsqrt(jnp.mean(x * x, axis=-1, keepdims=True) + EPS) * g


def _resident(shape):
    return pl.BlockSpec(shape, lambda *_: (0,) * len(shape), pipeline_mode=pl.Buffered(1))


def _mod_kernel(ct_ref, w_ref, b_ref, lq1_ref, lk1_ref, lq2_ref, lk2_ref, mod_ref, lam_ref):
    ct = ct_ref[...]
    act = ct * jax.nn.sigmoid(ct)
    w = w_ref[...]
    for b in range(ct.shape[1]):
        row = jnp.sum(act[:, b:b + 1] * w, axis=0, keepdims=True)
        mod_ref[b:b + 1, :] = row + b_ref[...]

    @pl.when(pl.program_id(0) == 0)
    def _():
        s1 = jnp.sum(lq1_ref[...] * lk1_ref[...], axis=-1, keepdims=True)
        s2 = jnp.sum(lq2_ref[...] * lk2_ref[...], axis=-1, keepdims=True)
        lam = jnp.exp(s1) - jnp.exp(s2) + LAMBDA_INIT
        lam_ref[...] = jnp.broadcast_to(lam, lam_ref.shape)


def _modulation(c, w_ada, b_ada, lq1, lk1, lq2, lk2):
    B, D = c.shape
    N = w_ada.shape[1]
    tn = 512
    row = lambda a: a.reshape(1, -1)
    small = pl.BlockSpec((1, DIFF_QK_DIM), lambda n: (0, 0))
    return pl.pallas_call(
        _mod_kernel,
        grid=(N // tn,),
        in_specs=[
            pl.BlockSpec((D, B), lambda n: (0, 0)),
            pl.BlockSpec((D, tn), lambda n: (0, n)),
            pl.BlockSpec((1, tn), lambda n: (0, n)),
            small, small, small, small,
        ],
        out_specs=[
            pl.BlockSpec((B, tn), lambda n: (0, n)),
            pl.BlockSpec((1, LANES), lambda n: (0, 0)),
        ],
        out_shape=[
            jax.ShapeDtypeStruct((B, N), F32),
            jax.ShapeDtypeStruct((1, LANES), F32),
        ],
        compiler_params=pltpu.CompilerParams(
            dimension_semantics=("arbitrary",), vmem_limit_bytes=VMEM_LIMIT),
        name="adaln_mod",
    )(c.T, w_ada, row(b_ada), row(lq1), row(lk1), row(lq2), row(lk2))


def _rope(z, cos, sin_signed, first_half):
    partner = jnp.where(first_half,
                        pltpu.roll(z, LANES - DIFF_QK_DIM // 2, axis=1),
                        pltpu.roll(z, DIFF_QK_DIM // 2, axis=1))
    return z * cos + partner * sin_signed


def _in_proj_kernel(pos_ref, x_ref, mod_ref, g1_ref, w_ref, lng_ref, lnb_ref, ws_ref, bs_ref,
                    freq_ref, a_ref, gb_ref, q_ref, k_ref, vt_ref, h_sc, u_sc, sv_sc):
    tm = x_ref.shape[0]
    mod = mod_ref[0]
    sh1 = mod[:, 0:D_MODEL]
    sc1 = mod[:, D_MODEL:2 * D_MODEL]
    h_sc[...] = (_rms(x_ref[...], g1_ref[...]) * (1.0 + sc1) + sh1).astype(BF16)

    def proj(group):
        cols = slice(group * D_MODEL, (group + 1) * D_MODEL)
        return jnp.dot(h_sc[...], w_ref[:, cols], preferred_element_type=F32)

    def gmlp_u(z):
        u_sc[...] = jax.nn.gelu(z)

    def gmlp_v(z):
        g = jax.nn.gelu(z)
        gc = g - jnp.mean(g, axis=-1, keepdims=True)
        var = jnp.mean(gc * gc, axis=-1, keepdims=True)
        vln = (gc * lax.rsqrt(var + EPS) * lng_ref[...] + lnb_ref[...]).astype(BF16)
        for grp in range(GMLP_GROUPS):
            cols = slice(grp * CHUNK, (grp + 1) * CHUNK)
            w_s = ws_ref[grp]
            b_s = bs_ref[grp]
            for n in range(tm // CHUNK):
                rows = slice(n * CHUNK, (n + 1) * CHUNK)
                sv_sc[rows, cols] = jnp.dot(w_s, vln[rows, cols],
                                            preferred_element_type=F32) + b_s

    def gate_a(z):
        a_ref[...] = jax.nn.sigmoid(z) * (u_sc[...] * sv_sc[...])

    ang_t = freq_ref[...] * pos_ref[...]
    cos_t = jnp.cos(ang_t)
    sin_t = jnp.sin(ang_t)
    maps_per_slab = LANES // DIFF_QK_DIM
    cos = jnp.concatenate([cos_t, cos_t] * maps_per_slab, axis=0).T
    sin_signed = jnp.concatenate([-sin_t, sin_t] * maps_per_slab, axis=0).T
    lane = lax.broadcasted_iota(jnp.int32, (tm, LANES), 1)
    first_half = (lane % DIFF_QK_DIM) < (DIFF_QK_DIM // 2)

    def rope_store(out_ref, scale):
        def store(z):
            for s in range(D_MODEL // LANES):
                cols = slice(s * LANES, (s + 1) * LANES)
                r = _rope(z[:, cols], cos, sin_signed, first_half)
                if scale != 1.0:
                    r = r * scale
                out_ref[:, cols] = r.astype(out_ref.dtype)
        return store

    def v_attn(z):
        for h in range(DIFF_HEADS):
            vt_ref[h] = z[:, h * DIFF_VDIM:(h + 1) * DIFF_VDIM].T.astype(vt_ref.dtype)

    def gate_b(z):
        gb_ref[...] = jax.nn.sigmoid(z)

    stages = [
        (COL_U, gmlp_u), (COL_V, gmlp_v), (COL_GATE_A, gate_a),
        (COL_Q, rope_store(q_ref, DIFF_QK_DIM ** -0.5 * LOG2E)), (COL_K, rope_store(k_ref, 1.0)),
        (COL_VATT, v_attn), (COL_GATE_B, gate_b),
    ]
    z_next = proj(stages[0][0])
    for idx, (_, epilogue) in enumerate(stages):
        z = z_next
        if idx + 1 < len(stages):
            z_next = proj(stages[idx + 1][0])
        epilogue(z)


def _in_proj(pos, x2, mod3, g1, w_in, ln_g, ln_b, w_s, b_s, tiles_per_batch):
    T, D = x2.shape
    tm = TOKEN_TILE
    freq = ROPE_THETA ** (-jnp.arange(0, DIFF_QK_DIM, 2, dtype=F32) / DIFF_QK_DIM)
    freq = freq.reshape(DIFF_QK_DIM // 2, 1)
    tok = lambda i: (i, 0)
    out_tok = pl.BlockSpec((tm, D), tok)
    return pl.pallas_call(
        _in_proj_kernel,
        grid=(T // tm,),
        in_specs=[
            pl.BlockSpec((None, 1, tm), lambda i: (i, 0, 0)),
            pl.BlockSpec((tm, D), tok),
            pl.BlockSpec((1, 1, N_MOD * D), lambda i: (i // tiles_per_batch, 0, 0)),
            _resident((1, D)),
            _resident(w_in.shape),
            _resident((1, D)),
            _resident((1, D)),
            _resident((GMLP_GROUPS, CHUNK, CHUNK)),
            _resident((GMLP_GROUPS, CHUNK, CHUNK)),
            _resident((DIFF_QK_DIM // 2, 1)),
        ],
        out_specs=[
            out_tok, out_tok, out_tok, out_tok,
            pl.BlockSpec((None, DIFF_HEADS, None, DIFF_VDIM, tm),
                         lambda i: (i // tiles_per_batch, 0, i % tiles_per_batch, 0, 0)),
        ],
        out_shape=[
            jax.ShapeDtypeStruct((T, D), F32),
            jax.ShapeDtypeStruct((T, D), F32),
            jax.ShapeDtypeStruct((T, D), BF16),
            jax.ShapeDtypeStruct((T, D), BF16),
            jax.ShapeDtypeStruct((T // (tm * tiles_per_batch), DIFF_HEADS, tiles_per_batch,
                                  DIFF_VDIM, tm), BF16),
        ],
        scratch_shapes=[
            pltpu.VMEM((tm, D), BF16),
            pltpu.VMEM((tm, D), F32),
            pltpu.VMEM((tm, D), F32),
        ],
        compiler_params=pltpu.CompilerParams(
            dimension_semantics=("arbitrary",), vmem_limit_bytes=VMEM_LIMIT_RESIDENT),
        name="in_proj",
    )(pos, x2, mod3, g1, w_in, ln_g, ln_b, w_s, b_s, freq)


def _attn_kernel(lam_ref, q_ref, k_ref, vt_ref, g_ref, o_ref, q12_sc, acc_sc):
    tq, hd = q_ref.shape
    n_kt, _, tk = vt_ref.shape
    n_blk = 2 * tq // Q_BLOCK

    q = q_ref[...]
    lane = lax.broadcasted_iota(jnp.int32, q.shape, 1)
    zero = jnp.zeros_like(q)
    q12_sc[0:tq, :] = jnp.where(lane < DIFF_QK_DIM, q, zero)
    q12_sc[tq:2 * tq, :] = jnp.where(lane >= DIFF_QK_DIM, q, zero)
    acc_sc[...] = jnp.zeros(acc_sc.shape, F32)
    ones_rows = jnp.ones((SUM_ROWS, tk), BF16)

    def scores(kt, c):
        rows = slice(c * Q_BLOCK, (c + 1) * Q_BLOCK)
        st = lax.dot_general(k_ref[kt * tk:(kt + 1) * tk, :], q12_sc[rows, :],
                             (((1,), (1,)), ((), ())),
                             preferred_element_type=F32)
        return st, jnp.max(st, axis=0, keepdims=True)

    blocks = [(kt, c) for kt in range(n_kt) for c in range(n_blk)]
    m_run = [jnp.full((1, Q_BLOCK), -jnp.inf, F32) for _ in range(n_blk)]
    pending = [scores(*blk) for blk in blocks[:LOOKAHEAD]]
    for idx, (kt, c) in enumerate(blocks):
        rows = slice(c * Q_BLOCK, (c + 1) * Q_BLOCK)
        st, m_cur = pending.pop(0)
        if idx + LOOKAHEAD < len(blocks):
            pending.append(scores(*blocks[idx + LOOKAHEAD]))
        m_new = jnp.maximum(m_run[c], m_cur)
        alpha = jnp.exp2(m_run[c] - m_new)
        pt = jnp.exp2(st - m_new).astype(BF16)
        v_ext = jnp.concatenate([vt_ref[kt], ones_rows], axis=0)
        acc_sc[:, rows] = alpha * acc_sc[:, rows] + jnp.dot(
            v_ext, pt, preferred_element_type=F32)
        m_run[c] = m_new

    inv_l = 1.0 / acc_sc[hd:hd + 1, :]
    acc = acc_sc[0:hd, :] * inv_l
    ot = acc[:, 0:tq] - lam_ref[0:1, 0:1] * acc[:, tq:2 * tq]
    ms = jnp.mean(ot * ot, axis=0, keepdims=True)
    ot = ot * lax.rsqrt(ms + EPS) * (g_ref[...] * (1.0 - LAMBDA_INIT))
    o_ref[...] = ot.T.astype(o_ref.dtype)


def _attention(lam, q, k, vt, subln_g, B, S):
    T, D = q.shape
    tq = Q_TILE
    nq = S // tq
    hd = DIFF_VDIM
    n_kt, tk = vt.shape[2], vt.shape[4]
    return pl.pallas_call(
        _attn_kernel,
        grid=(B, DIFF_HEADS, nq),
        in_specs=[
            pl.BlockSpec((1, LANES), lambda b, h, i: (0, 0)),
            pl.BlockSpec((tq, hd), lambda b, h, i: (b * nq + i, h)),
            pl.BlockSpec((S, hd), lambda b, h, i: (b, h)),
            pl.BlockSpec((None, None, n_kt, hd, tk),
                         lambda b, h, i: (b, h, 0, 0, 0)),
            pl.BlockSpec((hd, 1), lambda b, h, i: (0, 0)),
        ],
        out_specs=pl.BlockSpec((tq, hd), lambda b, h, i: (b * nq + i, h)),
        out_shape=jax.ShapeDtypeStruct((T, D), F32),
        scratch_shapes=[
            pltpu.VMEM((2 * tq, hd), BF16),
            pltpu.VMEM((hd + SUM_ROWS, 2 * tq), F32),
        ],
        compiler_params=pltpu.CompilerParams(
            dimension_semantics=("arbitrary", "arbitrary", "arbitrary"),
            vmem_limit_bytes=VMEM_LIMIT),
        name="diff_attn",
    )(lam, q, k, vt, subln_g)


def _out_ffn_kernel(x_ref, a_ref, gb_ref, bb_ref, mod_ref, wout_ref, g2_ref, w1_ref, w2_ref,
                    gf_ref, o_ref):
    mod = mod_ref[0]
    gt1 = mod[:, 2 * D_MODEL:3 * D_MODEL]
    sh2 = mod[:, 3 * D_MODEL:4 * D_MODEL]
    sc2 = mod[:, 4 * D_MODEL:5 * D_MODEL]
    gt2 = mod[:, 5 * D_MODEL:6 * D_MODEL]

    merged = a_ref[...] + gb_ref[...] * bb_ref[...]
    x1 = x_ref[...] + gt1 * jnp.dot(merged.astype(BF16), wout_ref[...],
                                    preferred_element_type=F32)
    h2 = (_rms(x1, g2_ref[...]) * (1.0 + sc2) + sh2).astype(BF16)
    ff = None
    for c in range(D_FF // D_MODEL):
        cols = slice(c * D_MODEL, (c + 1) * D_MODEL)
        hid = jnp.dot(h2, w1_ref[:, cols], preferred_element_type=F32)
        hid = jnp.square(jnp.maximum(hid, 0.0)).astype(BF16)
        part = jnp.dot(hid, w2_ref[cols, :], preferred_element_type=F32)
        ff = part if ff is None else ff + part
    x2 = x1 + gt2 * ff
    o_ref[...] = _rms(x2, gf_ref[...])


def _out_ffn(x2, a, gb, bb, mod3, w_out, g2, w1, w2, gf, tiles_per_batch):
    T, D = x2.shape
    tm = TOKEN_TILE
    tok = pl.BlockSpec((tm, D), lambda i: (i, 0))
    return pl.pallas_call(
        _out_ffn_kernel,
        grid=(T // tm,),
        in_specs=[
            tok, tok, tok, tok,
            pl.BlockSpec((1, 1, N_MOD * D), lambda i: (i // tiles_per_batch, 0, 0)),
            _resident((D, D)),
            _resident((1, D)),
            _resident((D, D_FF)),
            _resident((D_FF, D)),
            _resident((1, D)),
        ],
        out_specs=tok,
        out_shape=jax.ShapeDtypeStruct((T, D), F32),
        compiler_params=pltpu.CompilerParams(
            dimension_semantics=("arbitrary",), vmem_limit_bytes=VMEM_LIMIT),
        name="out_ffn",
    )(x2, a, gb, bb, mod3, w_out, g2, w1, w2, gf)


def kernel(x, c, positions, w_ada, b_ada, g_norm1, w_in, gmlp_ln_g, gmlp_ln_b, w_spatial,
           b_spatial, lambda_q1, lambda_k1, lambda_q2, lambda_k2, subln_g, w_out, g_norm2,
           w_ff1, w_ff2, g_final):
    B, S, D = x.shape
    T = B * S
    assert D == D_MODEL and S % TOKEN_TILE == 0 and S % Q_TILE == 0
    assert w_ada.shape[0] == 1, "single-layer block"
    tiles_per_batch = S // TOKEN_TILE
    row = lambda a: a.reshape(1, -1)

    mod, lam = _modulation(c, w_ada[0], b_ada[0], lambda_q1[0], lambda_k1[0],
                           lambda_q2[0], lambda_k2[0])
    mod3 = mod.reshape(B, 1, N_MOD * D)
    x2 = x.reshape(T, D)
    pos = positions.reshape(T // TOKEN_TILE, 1, TOKEN_TILE).astype(F32)
    bias_rows = jnp.broadcast_to(b_spatial[0][:, :, None], (GMLP_GROUPS, CHUNK, CHUNK))

    a, gb, q, k, vt = _in_proj(pos, x2, mod3, row(g_norm1[0]), w_in[0].astype(BF16),
                               row(gmlp_ln_g[0]), row(gmlp_ln_b[0]),
                               w_spatial[0].astype(BF16), bias_rows, tiles_per_batch)
    bb = _attention(lam, q, k, vt, subln_g[0].reshape(-1, 1), B, S)
    out = _out_ffn(x2, a, gb, bb, mod3, w_out[0].astype(BF16), row(g_norm2[0]),
                   w_ff1[0].astype(BF16), w_ff2[0].astype(BF16), row(g_final),
                   tiles_per_batch)
    return out.reshape(B, S, D)
```

```python
import math

import jax
import jax.numpy as jnp
from jax import lax
from jax.experimental import pallas as pl
from jax.experimental.pallas import tpu as pltpu

D_MODEL = 1024
GMLP_GROUPS = 8
CHUNK = 128
DIFF_HEADS = 8
DIFF_VDIM = 128
DIFF_QK_DIM = 64
D_FF = 4 * D_MODEL
N_MOD = 6
ROPE_THETA = 10000.0
EPS = 1e-6
LAMBDA_INIT = 0.8 - 0.6 * math.exp(-0.3 * 0)
COL_U, COL_V, COL_Q, COL_K, COL_VATT, COL_GATE_A, COL_GATE_B = range(7)

LANES = 128
TOKEN_TILE = 512
Q_TILE = 2048
K_TILE = TOKEN_TILE
Q_BLOCK = 256
LOOKAHEAD = 4
SUM_ROWS = 16
LOG2E = math.log2(math.e)
VMEM_LIMIT = 48 * 1024 * 1024
VMEM_LIMIT_RESIDENT = 56 * 1024 * 1024

F32 = jnp.float32
BF16 = jnp.bfloat16


def _rms(x, g):
    return x * lax.rsqrt(jnp.mean(x * x, axis=-1, keepdims=True) + EPS) * g


def _resident(shape):
    return pl.BlockSpec(shape, lambda *_: (0,) * len(shape), pipeline_mode=pl.Buffered(1))


def _mod_kernel(ct_ref, w_ref, b_ref, lq1_ref, lk1_ref, lq2_ref, lk2_ref, mod_ref, lam_ref):
    ct = ct_ref[...]
    act = ct * jax.nn.sigmoid(ct)
    w = w_ref[...]
    for b in range(ct.shape[1]):
        row = jnp.sum(act[:, b:b + 1] * w, axis=0, keepdims=True)
        mod_ref[b:b + 1, :] = row + b_ref[...]

    @pl.when(pl.program_id(0) == 0)
    def _():
        s1 = jnp.sum(lq1_ref[...] * lk1_ref[...], axis=-1, keepdims=True)
        s2 = jnp.sum(lq2_ref[...] * lk2_ref[...], axis=-1, keepdims=True)
        lam = jnp.exp(s1) - jnp.exp(s2) + LAMBDA_INIT
        lam_ref[...] = jnp.broadcast_to(lam, lam_ref.shape)


def _modulation(c, w_ada, b_ada, lq1, lk1, lq2, lk2):
    B, D = c.shape
    N = w_ada.shape[1]
    tn = 512
    row = lambda a: a.reshape(1, -1)
    small = pl.BlockSpec((1, DIFF_QK_DIM), lambda n: (0, 0))
    return pl.pallas_call(
        _mod_kernel,
        grid=(N // tn,),
        in_specs=[
            pl.BlockSpec((D, B), lambda n: (0, 0)),
            pl.BlockSpec((D, tn), lambda n: (0, n)),
            pl.BlockSpec((1, tn), lambda n: (0, n)),
            small, small, small, small,
        ],
        out_specs=[
            pl.BlockSpec((B, tn), lambda n: (0, n)),
            pl.BlockSpec((1, LANES), lambda n: (0, 0)),
        ],
        out_shape=[
            jax.ShapeDtypeStruct((B, N), F32),
            jax.ShapeDtypeStruct((1, LANES), F32),
        ],
        compiler_params=pltpu.CompilerParams(
            dimension_semantics=("arbitrary",), vmem_limit_bytes=VMEM_LIMIT),
        name="adaln_mod",
    )(c.T, w_ada, row(b_ada), row(lq1), row(lk1), row(lq2), row(lk2))


def _rope(z, cos, sin_signed, first_half):
    partner = jnp.where(first_half,
                        pltpu.roll(z, LANES - DIFF_QK_DIM // 2, axis=1),
                        pltpu.roll(z, DIFF_QK_DIM // 2, axis=1))
    return z * cos + partner * sin_signed


def _in_proj_kernel(pos_ref, x_ref, xn_ref, mod_ref, modn_ref, g1_ref, w_ref, lng_ref, lnb_ref,
                    ws_ref, bs_ref, freq_ref, a_ref, gb_ref, q_ref, k_ref, vt_ref,
                    h_sc, u_sc, sv_sc):
    tm = x_ref.shape[0]

    def normed(x_tile_ref, mod_tile_ref):
        mod = mod_tile_ref[0]
        sh1 = mod[:, 0:D_MODEL]
        sc1 = mod[:, D_MODEL:2 * D_MODEL]
        return (_rms(x_tile_ref[...], g1_ref[...] * (1.0 + sc1)) + sh1).astype(BF16)

    @pl.when(pl.program_id(0) == 0)
    def _():
        h_sc[...] = normed(x_ref, mod_ref)

    def proj(group):
        cols = slice(group * D_MODEL, (group + 1) * D_MODEL)
        return jnp.dot(h_sc[...], w_ref[:, cols], preferred_element_type=F32)

    def gmlp_u(z):
        u_sc[...] = jax.nn.gelu(z)

    def gmlp_v(z):
        g = jax.nn.gelu(z)
        gc = g - jnp.mean(g, axis=-1, keepdims=True)
        var = jnp.mean(gc * gc, axis=-1, keepdims=True)
        vln = (gc * lax.rsqrt(var + EPS) * lng_ref[...] + lnb_ref[...]).astype(BF16)
        for grp in range(GMLP_GROUPS):
            cols = slice(grp * CHUNK, (grp + 1) * CHUNK)
            w_s = ws_ref[grp]
            b_s = bs_ref[grp]
            for n in range(tm // CHUNK):
                rows = slice(n * CHUNK, (n + 1) * CHUNK)
                sv_sc[rows, cols] = jnp.dot(w_s, vln[rows, cols],
                                            preferred_element_type=F32) + b_s

    def gate_a(z):
        a_ref[...] = jax.nn.sigmoid(z) * (u_sc[...] * sv_sc[...])

    ang_t = freq_ref[...] * pos_ref[...]
    cos_t = jnp.cos(ang_t)
    sin_t = jnp.sin(ang_t)
    maps_per_slab = LANES // DIFF_QK_DIM
    cos = jnp.concatenate([cos_t, cos_t] * maps_per_slab, axis=0).T
    sin_signed = jnp.concatenate([-sin_t, sin_t] * maps_per_slab, axis=0).T
    lane = lax.broadcasted_iota(jnp.int32, (tm, LANES), 1)
    first_half = (lane % DIFF_QK_DIM) < (DIFF_QK_DIM // 2)

    def rope_store(out_ref, scale):
        cos_s, sin_s = (cos, sin_signed) if scale == 1.0 else (cos * scale, sin_signed * scale)

        def store(z):
            for s in range(D_MODEL // LANES):
                cols = slice(s * LANES, (s + 1) * LANES)
                out_ref[:, cols] = _rope(z[:, cols], cos_s, sin_s, first_half).astype(
                    out_ref.dtype)
        return store

    def v_attn(z):
        for h in range(DIFF_HEADS):
            vt_ref[h] = z[:, h * DIFF_VDIM:(h + 1) * DIFF_VDIM].T.astype(vt_ref.dtype)

    def gate_b(z):
        gb_ref[...] = jax.nn.sigmoid(z)

    stages = [
        (COL_U, gmlp_u), (COL_V, gmlp_v), (COL_GATE_A, gate_a),
        (COL_Q, rope_store(q_ref, DIFF_QK_DIM ** -0.5 * LOG2E)), (COL_K, rope_store(k_ref, 1.0)),
        (COL_VATT, v_attn), (COL_GATE_B, gate_b),
    ]
    z_next = proj(stages[0][0])
    for idx, (_, epilogue) in enumerate(stages):
        z = z_next
        if idx + 1 < len(stages):
            z_next = proj(stages[idx + 1][0])
        epilogue(z)
    h_sc[...] = normed(xn_ref, modn_ref)


def _in_proj(pos, x2, mod3, g1, w_in, ln_g, ln_b, w_s, b_s, tiles_per_batch):
    T, D = x2.shape
    tm = TOKEN_TILE
    freq = ROPE_THETA ** (-jnp.arange(0, DIFF_QK_DIM, 2, dtype=F32) / DIFF_QK_DIM)
    freq = freq.reshape(DIFF_QK_DIM // 2, 1)
    tok = lambda i: (i, 0)
    nxt = lambda i: (jnp.minimum(i + 1, T // tm - 1), 0)
    out_tok = pl.BlockSpec((tm, D), tok)
    return pl.pallas_call(
        _in_proj_kernel,
        grid=(T // tm,),
        in_specs=[
            pl.BlockSpec((None, 1, tm), lambda i: (i, 0, 0)),
            pl.BlockSpec((tm, D), tok),
            pl.BlockSpec((tm, D), nxt),
            pl.BlockSpec((1, 1, N_MOD * D), lambda i: (i // tiles_per_batch, 0, 0)),
            pl.BlockSpec((1, 1, N_MOD * D), lambda i: (nxt(i)[0] // tiles_per_batch, 0, 0)),
            _resident((1, D)),
            _resident(w_in.shape),
            _resident((1, D)),
            _resident((1, D)),
            _resident((GMLP_GROUPS, CHUNK, CHUNK)),
            _resident((GMLP_GROUPS, CHUNK, CHUNK)),
            _resident((DIFF_QK_DIM // 2, 1)),
        ],
        out_specs=[
            out_tok, out_tok, out_tok, out_tok,
            pl.BlockSpec((None, DIFF_HEADS, None, DIFF_VDIM, tm),
                         lambda i: (i // tiles_per_batch, 0, i % tiles_per_batch, 0, 0)),
        ],
        out_shape=[
            jax.ShapeDtypeStruct((T, D), F32),
            jax.ShapeDtypeStruct((T, D), F32),
            jax.ShapeDtypeStruct((T, D), BF16),
            jax.ShapeDtypeStruct((T, D), BF16),
            jax.ShapeDtypeStruct((T // (tm * tiles_per_batch), DIFF_HEADS, tiles_per_batch,
                                  DIFF_VDIM, tm), BF16),
        ],
        scratch_shapes=[
            pltpu.VMEM((tm, D), BF16),
            pltpu.VMEM((tm, D), F32),
            pltpu.VMEM((tm, D), F32),
        ],
        compiler_params=pltpu.CompilerParams(
            dimension_semantics=("arbitrary",), vmem_limit_bytes=VMEM_LIMIT_RESIDENT),
        name="in_proj",
    )(pos, x2, x2, mod3, mod3, g1, w_in, ln_g, ln_b, w_s, b_s, freq)


def _attn_kernel(lam_ref, q_ref, k_ref, vt_ref, g_ref, o_ref, q12_sc, acc_sc):
    tq, hd = q_ref.shape
    n_kt, _, tk = vt_ref.shape
    n_blk = 2 * tq // Q_BLOCK

    q = q_ref[...]
    lane = lax.broadcasted_iota(jnp.int32, q.shape, 1)
    zero = jnp.zeros_like(q)
    q12_sc[0:tq, :] = jnp.where(lane < DIFF_QK_DIM, q, zero)
    q12_sc[tq:2 * tq, :] = jnp.where(lane >= DIFF_QK_DIM, q, zero)
    acc_sc[...] = jnp.zeros(acc_sc.shape, F32)
    ones_rows = jnp.ones((SUM_ROWS, tk), BF16)

    def scores(kt, c):
        rows = slice(c * Q_BLOCK, (c + 1) * Q_BLOCK)
        st = lax.dot_general(k_ref[kt * tk:(kt + 1) * tk, :], q12_sc[rows, :],
                             (((1,), (1,)), ((), ())),
                             preferred_element_type=F32)
        return st, jnp.max(st, axis=0, keepdims=True)

    blocks = [(kt, c) for kt in range(n_kt) for c in range(n_blk)]
    m_run = [jnp.full((1, Q_BLOCK), -jnp.inf, F32) for _ in range(n_blk)]
    pending = [scores(*blk) for blk in blocks[:LOOKAHEAD]]
    for idx, (kt, c) in enumerate(blocks):
        rows = slice(c * Q_BLOCK, (c + 1) * Q_BLOCK)
        st, m_cur = pending.pop(0)
        if idx + LOOKAHEAD < len(blocks):
            pending.append(scores(*blocks[idx + LOOKAHEAD]))
        m_new = jnp.maximum(m_run[c], m_cur)
        alpha = jnp.exp2(m_run[c] - m_new)
        pt = jnp.exp2(st - m_new).astype(BF16)
        v_ext = jnp.concatenate([vt_ref[kt], ones_rows], axis=0)
        acc_sc[:, rows] = alpha * acc_sc[:, rows] + jnp.dot(
            v_ext, pt, preferred_element_type=F32)
        m_run[c] = m_new

    inv_l = 1.0 / acc_sc[hd:hd + 1, :]
    acc = acc_sc[0:hd, :] * inv_l
    ot = acc[:, 0:tq] - lam_ref[0:1, 0:1] * acc[:, tq:2 * tq]
    ms = jnp.mean(ot * ot, axis=0, keepdims=True)
    ot = ot * lax.rsqrt(ms + EPS) * (g_ref[...] * (1.0 - LAMBDA_INIT))
    o_ref[...] = ot.T.astype(o_ref.dtype)


def _attention(lam, q, k, vt, subln_g, B, S):
    T, D = q.shape
    tq = Q_TILE
    nq = S // tq
    hd = DIFF_VDIM
    n_kt, tk = vt.shape[2], vt.shape[4]
    return pl.pallas_call(
        _attn_kernel,
        grid=(B, DIFF_HEADS, nq),
        in_specs=[
            pl.BlockSpec((1, LANES), lambda b, h, i: (0, 0)),
            pl.BlockSpec((tq, hd), lambda b, h, i: (b * nq + i, h)),
            pl.BlockSpec((S, hd), lambda b, h, i: (b, h)),
            pl.BlockSpec((None, None, n_kt, hd, tk),
                         lambda b, h, i: (b, h, 0, 0, 0)),
            pl.BlockSpec((hd, 1), lambda b, h, i: (0, 0)),
        ],
        out_specs=pl.BlockSpec((tq, hd), lambda b, h, i: (b * nq + i, h)),
        out_shape=jax.ShapeDtypeStruct((T, D), F32),
        scratch_shapes=[
            pltpu.VMEM((2 * tq, hd), BF16),
            pltpu.VMEM((hd + SUM_ROWS, 2 * tq), F32),
        ],
        compiler_params=pltpu.CompilerParams(
            dimension_semantics=("arbitrary", "arbitrary", "arbitrary"),
            vmem_limit_bytes=VMEM_LIMIT),
        name="diff_attn",
    )(lam, q, k, vt, subln_g)


def _out_ffn_kernel(x_ref, a_ref, gb_ref, bb_ref, mod_ref, wout_ref, g2_ref, w1_ref, w2_ref,
                    gf_ref, o_ref):
    mod = mod_ref[0]
    gt1 = mod[:, 2 * D_MODEL:3 * D_MODEL]
    sh2 = mod[:, 3 * D_MODEL:4 * D_MODEL]
    sc2 = mod[:, 4 * D_MODEL:5 * D_MODEL]
    gt2 = mod[:, 5 * D_MODEL:6 * D_MODEL]

    merged = a_ref[...] + gb_ref[...] * bb_ref[...]
    x1 = x_ref[...] + gt1 * jnp.dot(merged.astype(BF16), wout_ref[...],
                                    preferred_element_type=F32)
    h2 = (_rms(x1, g2_ref[...]) * (1.0 + sc2) + sh2).astype(BF16)
    ff = None
    for c in range(D_FF // D_MODEL):
        cols = slice(c * D_MODEL, (c + 1) * D_MODEL)
        hid = jnp.dot(h2, w1_ref[:, cols], preferred_element_type=F32)
        hid = jnp.square(jnp.maximum(hid, 0.0)).astype(BF16)
        part = jnp.dot(hid, w2_ref[cols, :], preferred_element_type=F32)
        ff = part if ff is None else ff + part
    x2 = x1 + gt2 * ff
    o_ref[...] = _rms(x2, gf_ref[...])


def _out_ffn(x2, a, gb, bb, mod3, w_out, g2, w1, w2, gf, tiles_per_batch):
    T, D = x2.shape
    tm = TOKEN_TILE
    tok = pl.BlockSpec((tm, D), lambda i: (i, 0))
    return pl.pallas_call(
        _out_ffn_kernel,
        grid=(T // tm,),
        in_specs=[
            tok, tok, tok, tok,
            pl.BlockSpec((1, 1, N_MOD * D), lambda i: (i // tiles_per_batch, 0, 0)),
            _resident((D, D)),
            _resident((1, D)),
            _resident((D, D_FF)),
            _resident((D_FF, D)),
            _resident((1, D)),
        ],
        out_specs=tok,
        out_shape=jax.ShapeDtypeStruct((T, D), F32),
        compiler_params=pltpu.CompilerParams(
            dimension_semantics=("arbitrary",), vmem_limit_bytes=VMEM_LIMIT),
        name="out_ffn",
    )(x2, a, gb, bb, mod3, w_out, g2, w1, w2, gf)


def kernel(x, c, positions, w_ada, b_ada, g_norm1, w_in, gmlp_ln_g, gmlp_ln_b, w_spatial,
           b_spatial, lambda_q1, lambda_k1, lambda_q2, lambda_k2, subln_g, w_out, g_norm2,
           w_ff1, w_ff2, g_final):
    B, S, D = x.shape
    T = B * S
    assert D == D_MODEL and S % TOKEN_TILE == 0 and S % Q_TILE == 0
    assert w_ada.shape[0] == 1, "single-layer block"
    tiles_per_batch = S // TOKEN_TILE
    row = lambda a: a.reshape(1, -1)

    mod, lam = _modulation(c, w_ada[0], b_ada[0], lambda_q1[0], lambda_k1[0],
                           lambda_q2[0], lambda_k2[0])
    mod3 = mod.reshape(B, 1, N_MOD * D)
    x2 = x.reshape(T, D)
    pos = positions.reshape(T // TOKEN_TILE, 1, TOKEN_TILE).astype(F32)
    bias_rows = jnp.broadcast_to(b_spatial[0][:, :, None], (GMLP_GROUPS, CHUNK, CHUNK))

    a, gb, q, k, vt = _in_proj(pos, x2, mod3, row(g_norm1[0]), w_in[0].astype(BF16),
                               row(gmlp_ln_g[0]), row(gmlp_ln_b[0]),
                               w_spatial[0].astype(BF16), bias_rows, tiles_per_batch)
    bb = _attention(lam, q, k, vt, subln_g[0].reshape(-1, 1), B, S)
    out = _out_ffn(x2, a, gb, bb, mod3, w_out[0].astype(BF16), row(g_norm2[0]),
                   w_ff1[0].astype(BF16), w_ff2[0].astype(BF16), row(g_final),
                   tiles_per_batch)
    return out.reshape(B, S, D)
```

```python
import math

import jax
import jax.numpy as jnp
from jax import lax
from jax.experimental import pallas as pl
from jax.experimental.pallas import tpu as pltpu

D_MODEL = 1024
GMLP_GROUPS = 8
CHUNK = 128
DIFF_HEADS = 8
DIFF_VDIM = 128
DIFF_QK_DIM = 64
D_FF = 4 * D_MODEL
N_MOD = 6
ROPE_THETA = 10000.0
EPS = 1e-6
LAMBDA_INIT = 0.8 - 0.6 * math.exp(-0.3 * 0)
COL_U, COL_V, COL_Q, COL_K, COL_VATT, COL_GATE_A, COL_GATE_B = range(7)

LANES = 128
TOKEN_TILE = 512
Q_TILE = 2048
K_TILE = TOKEN_TILE
Q_BLOCK = 256
LOOKAHEAD = 4
SUM_ROWS = 16
LOG2E = math.log2(math.e)
VMEM_LIMIT = 48 * 1024 * 1024
VMEM_LIMIT_RESIDENT = 56 * 1024 * 1024

F32 = jnp.float32
BF16 = jnp.bfloat16


def _rms(x, g):
    return x * lax.rsqrt(jnp.mean(x * x, axis=-1, keepdims=True) + EPS) * g


def _resident(shape):
    return pl.BlockSpec(shape, lambda *_: (0,) * len(shape), pipeline_mode=pl.Buffered(1))


def _mod_kernel(ct_ref, w_ref, b_ref, lq1_ref, lk1_ref, lq2_ref, lk2_ref, mod_ref, lam_ref):
    ct = ct_ref[...]
    act = ct * jax.nn.sigmoid(ct)
    w = w_ref[...]
    for b in range(ct.shape[1]):
        row = jnp.sum(act[:, b:b + 1] * w, axis=0, keepdims=True)
        mod_ref[b:b + 1, :] = row + b_ref[...]

    @pl.when(pl.program_id(0) == 0)
    def _():
        s1 = jnp.sum(lq1_ref[...] * lk1_ref[...], axis=-1, keepdims=True)
        s2 = jnp.sum(lq2_ref[...] * lk2_ref[...], axis=-1, keepdims=True)
        lam = jnp.exp(s1) - jnp.exp(s2) + LAMBDA_INIT
        lam_ref[...] = jnp.broadcast_to(lam, lam_ref.shape)


def _modulation(c, w_ada, b_ada, lq1, lk1, lq2, lk2):
    B, D = c.shape
    N = w_ada.shape[1]
    tn = 512
    row = lambda a: a.reshape(1, -1)
    small = pl.BlockSpec((1, DIFF_QK_DIM), lambda n: (0, 0))
    return pl.pallas_call(
        _mod_kernel,
        grid=(N // tn,),
        in_specs=[
            pl.BlockSpec((D, B), lambda n: (0, 0)),
            pl.BlockSpec((D, tn), lambda n: (0, n)),
            pl.BlockSpec((1, tn), lambda n: (0, n)),
            small, small, small, small,
        ],
        out_specs=[
            pl.BlockSpec((B, tn), lambda n: (0, n)),
            pl.BlockSpec((1, LANES), lambda n: (0, 0)),
        ],
        out_shape=[
            jax.ShapeDtypeStruct((B, N), F32),
            jax.ShapeDtypeStruct((1, LANES), F32),
        ],
        compiler_params=pltpu.CompilerParams(
            dimension_semantics=("arbitrary",), vmem_limit_bytes=VMEM_LIMIT),
        name="adaln_mod",
    )(c.T, w_ada, row(b_ada), row(lq1), row(lk1), row(lq2), row(lk2))


def _rope(z, cos, sin_signed, first_half):
    partner = jnp.where(first_half,
                        pltpu.roll(z, LANES - DIFF_QK_DIM // 2, axis=1),
                        pltpu.roll(z, DIFF_QK_DIM // 2, axis=1))
    return z * cos + partner * sin_signed


def _in_proj_kernel(pos_ref, x_ref, xn_ref, mod_ref, modn_ref, g1_ref, w_ref, lng_ref, lnb_ref,
                    ws_ref, bs_ref, freq_ref, a_ref, gb_ref, q_ref, k_ref, vt_ref,
                    h_sc, u_sc, sv_sc):
    tm = x_ref.shape[0]

    def normed(x_tile_ref, mod_tile_ref):
        mod = mod_tile_ref[0]
        sh1 = mod[:, 0:D_MODEL]
        sc1 = mod[:, D_MODEL:2 * D_MODEL]
        return (_rms(x_tile_ref[...], g1_ref[...] * (1.0 + sc1)) + sh1).astype(BF16)

    @pl.when(pl.program_id(0) == 0)
    def _():
        h_sc[...] = normed(x_ref, mod_ref)

    def proj(group):
        cols = slice(group * D_MODEL, (group + 1) * D_MODEL)
        return jnp.dot(h_sc[...], w_ref[:, cols], preferred_element_type=F32)

    def gmlp_u(z):
        u_sc[...] = jax.nn.gelu(z)

    def gmlp_v(z):
        g = jax.nn.gelu(z)
        gc = g - jnp.mean(g, axis=-1, keepdims=True)
        var = jnp.mean(gc * gc, axis=-1, keepdims=True)
        vln = (gc * lax.rsqrt(var + EPS) * lng_ref[...] + lnb_ref[...]).astype(BF16)
        for grp in range(GMLP_GROUPS):
            cols = slice(grp * CHUNK, (grp + 1) * CHUNK)
            w_s = ws_ref[grp]
            b_s = bs_ref[grp]
            for n in range(tm // CHUNK):
                rows = slice(n * CHUNK, (n + 1) * CHUNK)
                sv_sc[rows, cols] = jnp.dot(w_s, vln[rows, cols],
                                            preferred_element_type=F32) + b_s

    def gate_a(z):
        a_ref[...] = jax.nn.sigmoid(z) * (u_sc[...] * sv_sc[...])

    ang_t = freq_ref[...] * pos_ref[...]
    cos_t = jnp.cos(ang_t)
    sin_t = jnp.sin(ang_t)
    maps_per_slab = LANES // DIFF_QK_DIM
    cos = jnp.concatenate([cos_t, cos_t] * maps_per_slab, axis=0).T
    sin_signed = jnp.concatenate([-sin_t, sin_t] * maps_per_slab, axis=0).T
    lane = lax.broadcasted_iota(jnp.int32, (tm, LANES), 1)
    first_half = (lane % DIFF_QK_DIM) < (DIFF_QK_DIM // 2)

    def rope_store(out_ref, scale):
        cos_s, sin_s = (cos, sin_signed) if scale == 1.0 else (cos * scale, sin_signed * scale)

        def store(z):
            for s in range(D_MODEL // LANES):
                cols = slice(s * LANES, (s + 1) * LANES)
                out_ref[:, cols] = _rope(z[:, cols], cos_s, sin_s, first_half).astype(
                    out_ref.dtype)
        return store

    def v_attn(z):
        for h in range(DIFF_HEADS):
            vt_ref[h] = z[:, h * DIFF_VDIM:(h + 1) * DIFF_VDIM].T.astype(vt_ref.dtype)

    def gate_b(z):
        gb_ref[...] = jax.nn.sigmoid(z)

    stages = [
        (COL_U, gmlp_u), (COL_V, gmlp_v), (COL_GATE_A, gate_a),
        (COL_Q, rope_store(q_ref, DIFF_QK_DIM ** -0.5 * LOG2E)), (COL_K, rope_store(k_ref, 1.0)),
        (COL_VATT, v_attn), (COL_GATE_B, gate_b),
    ]
    z_next = proj(stages[0][0])
    for idx, (_, epilogue) in enumerate(stages):
        z = z_next
        if idx + 1 < len(stages):
            z_next = proj(stages[idx + 1][0])
        epilogue(z)
    h_sc[...] = normed(xn_ref, modn_ref)


def _in_proj(pos, x2, mod3, g1, w_in, ln_g, ln_b, w_s, b_s, tiles_per_batch):
    T, D = x2.shape
    tm = TOKEN_TILE
    freq = ROPE_THETA ** (-jnp.arange(0, DIFF_QK_DIM, 2, dtype=F32) / DIFF_QK_DIM)
    freq = freq.reshape(DIFF_QK_DIM // 2, 1)
    tok = lambda i: (i, 0)
    nxt = lambda i: (jnp.minimum(i + 1, T // tm - 1), 0)
    out_tok = pl.BlockSpec((tm, D), tok)
    return pl.pallas_call(
        _in_proj_kernel,
        grid=(T // tm,),
        in_specs=[
            pl.BlockSpec((None, 1, tm), lambda i: (i, 0, 0)),
            pl.BlockSpec((tm, D), tok),
            pl.BlockSpec((tm, D), nxt),
            pl.BlockSpec((1, 1, N_MOD * D), lambda i: (i // tiles_per_batch, 0, 0)),
            pl.BlockSpec((1, 1, N_MOD * D), lambda i: (nxt(i)[0] // tiles_per_batch, 0, 0)),
            _resident((1, D)),
            _resident(w_in.shape),
            _resident((1, D)),
            _resident((1, D)),
            _resident((GMLP_GROUPS, CHUNK, CHUNK)),
            _resident((GMLP_GROUPS, CHUNK, CHUNK)),
            _resident((DIFF_QK_DIM // 2, 1)),
        ],
        out_specs=[
            out_tok, out_tok, out_tok, out_tok,
            pl.BlockSpec((None, DIFF_HEADS, None, DIFF_VDIM, tm),
                         lambda i: (i // tiles_per_batch, 0, i % tiles_per_batch, 0, 0)),
        ],
        out_shape=[
            jax.ShapeDtypeStruct((T, D), F32),
            jax.ShapeDtypeStruct((T, D), F32),
            jax.ShapeDtypeStruct((T, D), BF16),
            jax.ShapeDtypeStruct((T, D), BF16),
            jax.ShapeDtypeStruct((T // (tm * tiles_per_batch), DIFF_HEADS, tiles_per_batch,
                                  DIFF_VDIM, tm), BF16),
        ],
        scratch_shapes=[
            pltpu.VMEM((tm, D), BF16),
            pltpu.VMEM((tm, D), F32),
            pltpu.VMEM((tm, D), F32),
        ],
        compiler_params=pltpu.CompilerParams(
            dimension_semantics=("arbitrary",), vmem_limit_bytes=VMEM_LIMIT_RESIDENT),
        name="in_proj",
    )(pos, x2, x2, mod3, mod3, g1, w_in, ln_g, ln_b, w_s, b_s, freq)


def _attn_kernel(lam_ref, q_ref, k_ref, vt_ref, g_ref, o_ref, q12_sc, acc_sc):
    tq, hd = q_ref.shape
    n_kt, _, tk = vt_ref.shape
    n_blk = 2 * tq // Q_BLOCK

    q = q_ref[...]
    lane = lax.broadcasted_iota(jnp.int32, q.shape, 1)
    zero = jnp.zeros_like(q)
    q12_sc[0:tq, :] = jnp.where(lane < DIFF_QK_DIM, q, zero)
    q12_sc[tq:2 * tq, :] = jnp.where(lane >= DIFF_QK_DIM, q, zero)
    ones_rows = jnp.ones((SUM_ROWS, tk), BF16)

    def scores(kt, c):
        rows = slice(c * Q_BLOCK, (c + 1) * Q_BLOCK)
        st = lax.dot_general(k_ref[kt * tk:(kt + 1) * tk, :], q12_sc[rows, :],
                             (((1,), (1,)), ((), ())),
                             preferred_element_type=F32)
        return st, jnp.max(st, axis=0, keepdims=True)

    blocks = [(kt, c) for kt in range(n_kt) for c in range(n_blk)]
    m_run = [jnp.full((1, Q_BLOCK), -jnp.inf, F32) for _ in range(n_blk)]
    pending = [scores(*blk) for blk in blocks[:LOOKAHEAD]]
    for idx, (kt, c) in enumerate(blocks):
        rows = slice(c * Q_BLOCK, (c + 1) * Q_BLOCK)
        st, m_cur = pending.pop(0)
        if idx + LOOKAHEAD < len(blocks):
            pending.append(scores(*blocks[idx + LOOKAHEAD]))
        m_new = jnp.maximum(m_run[c], m_cur)
        alpha = jnp.exp2(m_run[c] - m_new)
        pt = jnp.exp2(st - m_new).astype(BF16)
        v_ext = jnp.concatenate([vt_ref[kt], ones_rows], axis=0)
        pv = jnp.dot(v_ext, pt, preferred_element_type=F32)
        acc_sc[:, rows] = pv if kt == 0 else alpha * acc_sc[:, rows] + pv
        m_run[c] = m_new

    inv_l = 1.0 / acc_sc[hd:hd + 1, :]
    acc = acc_sc[0:hd, :] * inv_l
    ot = acc[:, 0:tq] - lam_ref[0:1, 0:1] * acc[:, tq:2 * tq]
    ms = jnp.mean(ot * ot, axis=0, keepdims=True)
    ot = ot * lax.rsqrt(ms + EPS) * (g_ref[...] * (1.0 - LAMBDA_INIT))
    o_ref[...] = ot.T.astype(o_ref.dtype)


def _attention(lam, q, k, vt, subln_g, B, S):
    T, D = q.shape
    tq = Q_TILE
    nq = S // tq
    hd = DIFF_VDIM
    n_kt, tk = vt.shape[2], vt.shape[4]
    return pl.pallas_call(
        _attn_kernel,
        grid=(B, DIFF_HEADS, nq),
        in_specs=[
            pl.BlockSpec((1, LANES), lambda b, h, i: (0, 0)),
            pl.BlockSpec((tq, hd), lambda b, h, i: (b * nq + i, h)),
            pl.BlockSpec((S, hd), lambda b, h, i: (b, h)),
            pl.BlockSpec((None, None, n_kt, hd, tk),
                         lambda b, h, i: (b, h, 0, 0, 0)),
            pl.BlockSpec((hd, 1), lambda b, h, i: (0, 0)),
        ],
        out_specs=pl.BlockSpec((tq, hd), lambda b, h, i: (b * nq + i, h)),
        out_shape=jax.ShapeDtypeStruct((T, D), F32),
        scratch_shapes=[
            pltpu.VMEM((2 * tq, hd), BF16),
            pltpu.VMEM((hd + SUM_ROWS, 2 * tq), F32),
        ],
        compiler_params=pltpu.CompilerParams(
            dimension_semantics=("arbitrary", "arbitrary", "arbitrary"),
            vmem_limit_bytes=VMEM_LIMIT),
        name="diff_attn",
    )(lam, q, k, vt, subln_g)


def _out_ffn_kernel(x_ref, a_ref, gb_ref, bb_ref, mod_ref, wout_ref, g2_ref, w1_ref, w2_ref,
                    gf_ref, o_ref):
    mod = mod_ref[0]
    gt1 = mod[:, 2 * D_MODEL:3 * D_MODEL]
    sh2 = mod[:, 3 * D_MODEL:4 * D_MODEL]
    sc2 = mod[:, 4 * D_MODEL:5 * D_MODEL]
    gt2 = mod[:, 5 * D_MODEL:6 * D_MODEL]

    merged = a_ref[...] + gb_ref[...] * bb_ref[...]
    x1 = x_ref[...] + gt1 * jnp.dot(merged.astype(BF16), wout_ref[...],
                                    preferred_element_type=F32)
    h2 = (_rms(x1, g2_ref[...]) * (1.0 + sc2) + sh2).astype(BF16)
    ff = None
    for c in range(D_FF // D_MODEL):
        cols = slice(c * D_MODEL, (c + 1) * D_MODEL)
        hid = jnp.dot(h2, w1_ref[:, cols], preferred_element_type=F32)
        hid = jnp.square(jnp.maximum(hid, 0.0)).astype(BF16)
        part = jnp.dot(hid, w2_ref[cols, :], preferred_element_type=F32)
        ff = part if ff is None else ff + part
    x2 = x1 + gt2 * ff
    o_ref[...] = _rms(x2, gf_ref[...])


def _out_ffn(x2, a, gb, bb, mod3, w_out, g2, w1, w2, gf, tiles_per_batch):
    T, D = x2.shape
    tm = TOKEN_TILE
    tok = pl.BlockSpec((tm, D), lambda i: (i, 0))
    return pl.pallas_call(
        _out_ffn_kernel,
        grid=(T // tm,),
        in_specs=[
            tok, tok, tok, tok,
            pl.BlockSpec((1, 1, N_MOD * D), lambda i: (i // tiles_per_batch, 0, 0)),
            _resident((D, D)),
            _resident((1, D)),
            _resident((D, D_FF)),
            _resident((D_FF, D)),
            _resident((1, D)),
        ],
        out_specs=tok,
        out_shape=jax.ShapeDtypeStruct((T, D), F32),
        compiler_params=pltpu.CompilerParams(
            dimension_semantics=("arbitrary",), vmem_limit_bytes=VMEM_LIMIT),
        name="out_ffn",
    )(x2, a, gb, bb, mod3, w_out, g2, w1, w2, gf)


def kernel(x, c, positions, w_ada, b_ada, g_norm1, w_in, gmlp_ln_g, gmlp_ln_b, w_spatial,
           b_spatial, lambda_q1, lambda_k1, lambda_q2, lambda_k2, subln_g, w_out, g_norm2,
           w_ff1, w_ff2, g_final):
    B, S, D = x.shape
    T = B * S
    assert D == D_MODEL and S % TOKEN_TILE == 0 and S % Q_TILE == 0
    assert w_ada.shape[0] == 1, "single-layer block"
    tiles_per_batch = S // TOKEN_TILE
    row = lambda a: a.reshape(1, -1)

    mod, lam = _modulation(c, w_ada[0], b_ada[0], lambda_q1[0], lambda_k1[0],
                           lambda_q2[0], lambda_k2[0])
    mod3 = mod.reshape(B, 1, N_MOD * D)
    x2 = x.reshape(T, D)
    pos = positions.reshape(T // TOKEN_TILE, 1, TOKEN_TILE).astype(F32)
    bias_rows = jnp.broadcast_to(b_spatial[0][:, :, None], (GMLP_GROUPS, CHUNK, CHUNK))

    a, gb, q, k, vt = _in_proj(pos, x2, mod3, row(g_norm1[0]), w_in[0].astype(BF16),
                               row(gmlp_ln_g[0]), row(gmlp_ln_b[0]),
                               w_spatial[0].astype(BF16), bias_rows, tiles_per_batch)
    bb = _attention(lam, q, k, vt, subln_g[0].reshape(-1, 1), B, S)
    out = _out_ffn(x2, a, gb, bb, mod3, w_out[0].astype(BF16), row(g_norm2[0]),
                   w_ff1[0].astype(BF16), w_ff2[0].astype(BF16), row(g_final),
                   tiles_per_batch)
    return out.reshape(B, S, D)
```
